```python
import jax, jax.numpy as jnp
from jax import lax
import numpy as np

D_MODEL = 2048
BATCH = 16
SEQ = 256
DEPTH = 2
DEC_BATCH = 8
DEC_SEQ = 4096
PAST_LEN = 256

GRID_W = 64
H_A = 16
HEAD_A = 64
W_A = H_A * HEAD_A
H_B = 16
W_B = 1024
BLK_B = W_B // H_B
CONV_W = 4
CONV_PAD_L = 2
CONV_PAD_R = CONV_W - 1 - CONV_PAD_L
LORA_DEC = 96
LORA_ICLR = 96
LORA_VRES = 64
LRU_C = 8.0
RMS_EPS = 1e-6
GN_EPS = 64e-5
N_IN = 4 * W_A + 2 * W_B + 2 * D_MODEL
SPLITS = (W_A, 2 * W_A, 3 * W_A, 4 * W_A, 4 * W_A + W_B, 4 * W_A + 2 * W_B, 4 * W_A + 2 * W_B + D_MODEL)

kernel_name = 'bidir_rwkv7_rglru_prefix_diffusion_step'


def rms_norm(x, g):
    xf = x.astype(jnp.float32)
    y = xf * lax.rsqrt(jnp.mean(xf * xf, axis=-1, keepdims=True) + RMS_EPS)
    return (y * g.astype(jnp.float32)).astype(x.dtype)


def to_heads(t):
    return t.reshape(t.shape[0], t.shape[1], H_A, HEAD_A)


def shift_1d(z):
    zp = jnp.pad(z, ((0, 0), (1, 1), (0, 0)))
    return 0.5 * (zp[:, :-2] + zp[:, 2:])


def shift_grid(z):
    bsz, n, ch = z.shape
    rows = n // GRID_W
    zp = jnp.pad(z.reshape(bsz, rows, GRID_W, ch), ((0, 0), (1, 1), (1, 1), (0, 0)))
    s = 0.25 * (zp[:, :-2, 1:-1] + zp[:, 2:, 1:-1] + zp[:, 1:-1, :-2] + zp[:, 1:-1, 2:])
    return s.reshape(bsz, n, ch)


def dwconv_centred(x, w, b):
    t = x.shape[1]
    xp = jnp.pad(x, ((0, 0), (CONV_PAD_L, CONV_PAD_R), (0, 0)))
    y = b + xp[:, 0:t] * w[0]
    for j in range(1, CONV_W):
        y = y + xp[:, j:j + t] * w[j]
    return y


def head_group_norm(y, g, b):
    bsz, t = y.shape[0], y.shape[1]
    yh = y.astype(jnp.float32).reshape(bsz, t, H_A, HEAD_A)
    mu = jnp.mean(yh, axis=-1, keepdims=True)
    var = jnp.mean(jnp.square(yh - mu), axis=-1, keepdims=True)
    yh = (yh - mu) * lax.rsqrt(var + GN_EPS)
    return yh.reshape(bsz, t, W_A) * g.astype(jnp.float32) + b.astype(jnp.float32)


def rwkv7_scan(r, w, k, v, kk, a, s0, reverse):
    def step(s, inp):
        r_t, w_t, k_t, v_t, kk_t, a_t = inp
        s_kk = jnp.einsum('bhvk,bhk->bhv', s, kk_t)
        s = s * w_t[:, :, None, :] - s_kk[..., None] * (kk_t * a_t)[:, :, None, :] + v_t[..., None] * k_t[:, :, None, :]
        return s, jnp.einsum('bhvk,bhk->bhv', s, r_t)
    xs = tuple(jnp.moveaxis(t.astype(jnp.float32), 1, 0) for t in (r, w, k, v, kk, a))
    s_fin, ys = lax.scan(step, s0.astype(jnp.float32), xs, reverse=reverse)
    return jnp.moveaxis(ys, 0, 1), s_fin


def linear_scan(a, u, h0, reverse):
    def step(h, au):
        h = au[0] * h + au[1]
        return h, h
    h_fin, hs = lax.scan(step, h0.astype(jnp.float32), (jnp.moveaxis(a, 1, 0), jnp.moveaxis(u, 1, 0)), reverse=reverse)
    return jnp.moveaxis(hs, 0, 1), h_fin


def rwkv_direction(xm, rh, k, vh, kk, dec_w0, dec_w1, dec_w2, iclr_w0, iclr_w1, iclr_w2, k_a, r_k, s0, reverse):
    w_log = -jax.nn.softplus(-(dec_w0 + jnp.tanh(xm @ dec_w1) @ dec_w2)) - 0.5
    decay = jnp.exp(-jnp.exp(w_log.astype(jnp.float32)))
    a = jax.nn.sigmoid(iclr_w0 + (xm @ iclr_w1) @ iclr_w2)
    kd = k * (1 + (a - 1) * k_a)
    kdh = to_heads(kd)
    y, s_fin = rwkv7_scan(rh, to_heads(decay), kdh, vh, kk, to_heads(a), s0, reverse)
    bonus = jnp.sum((rh * kdh * r_k).astype(jnp.float32), axis=-1, keepdims=True) * vh.astype(jnp.float32)
    return y, bonus, s_fin


def rglru_direction(xc, gr_w, gr_b, gi_w, gi_b, lam, h0, reverse):
    bsz, t = xc.shape[0], xc.shape[1]
    xblk = xc.reshape(bsz, t, H_B, BLK_B)
    rg = jax.nn.sigmoid(jnp.einsum('bthi,hij->bthj', xblk, gr_w).reshape(bsz, t, W_B) + gr_b)
    ig = jax.nn.sigmoid(jnp.einsum('bthi,hij->bthj', xblk, gi_w).reshape(bsz, t, W_B) + gi_b)
    log_a = (-LRU_C * jax.nn.softplus(-lam) * rg).astype(jnp.float32)
    a = jnp.exp(log_a)
    u = jnp.sqrt(-jnp.expm1(2.0 * log_a)) * (ig * xc).astype(jnp.float32)
    return linear_scan(a, u, h0, reverse)


def mixer(xm, p, shift_fn, s_rwkv0, h_lru0, v_first):
    dt = xm.dtype
    z = xm @ p['w_in']
    r, k, v, g_a, x_b, g_b, m_a, m_b = jnp.split(z, SPLITS, axis=-1)
    mu = p['mu_rkv']
    r = r + (shift_fn(r) - r) * mu[0]
    k = k + (shift_fn(k) - k) * mu[1]
    v = v + (shift_fn(v) - v) * mu[2]
    if 'vres_w0' in p:
        v = v + (v_first - v) * jax.nn.sigmoid(p['vres_w0'] + (xm @ p['vres_w1']) @ p['vres_w2'])
    rh, vh = to_heads(r), to_heads(v)
    kk = to_heads((k * p['k_k']).astype(jnp.float32))
    kk = kk / jnp.maximum(jnp.sqrt(jnp.sum(kk * kk, axis=-1, keepdims=True)), 1e-12)
    dirs_a = []
    for d in range(2):
        dirs_a.append(rwkv_direction(xm, rh, k, vh, kk, p['dec_w0'][d], p['dec_w1'][d], p['dec_w2'][d],
                                     p['iclr_w0'][d], p['iclr_w1'][d], p['iclr_w2'][d], p['k_a'], p['r_k'],
                                     s_rwkv0[:, d], d == 1))
    (y_f, bo_f, s_f), (y_bk, bo_bk, s_bk) = dirs_a
    bsz, t = xm.shape[0], xm.shape[1]
    y_rwkv = head_group_norm((y_f + y_bk).reshape(bsz, t, W_A), p['lnx_g'], p['lnx_b']) + (bo_f + bo_bk).reshape(bsz, t, W_A)
    y_a = (y_rwkv.astype(dt) * jax.nn.silu(g_a)) @ p['w_out_a']

    xc = dwconv_centred(x_b, p['conv_w'], p['conv_b'])
    dirs_b = []
    for d in range(2):
        dirs_b.append(rglru_direction(xc, p['gr_w'][d], p['gr_b'][d], p['gi_w'][d], p['gi_b'][d],
                                      p['lru_lambda'][d], h_lru0[:, d], d == 1))
    (h_f, hf_f), (h_bk, hf_bk) = dirs_b
    y_b = ((h_f + h_bk).astype(dt) * jax.nn.silu(g_b)) @ p['w_out_b']

    merged = jax.nn.sigmoid(m_a) * y_a + jax.nn.sigmoid(m_b) * y_b
    out = merged @ p['w_out']
    return out, jnp.stack([s_f, s_bk], axis=1), jnp.stack([hf_f, hf_bk], axis=1), v


def trunk(x, cond, shift_fn, rwkv_init, lru_init, layers):
    v_first = None
    s_out, h_out = [], []
    for l in range(DEPTH):
        p = layers[l]
        mod = jax.nn.silu(cond) @ p['ada_w'] + p['ada_b']
        shift, scale, gate = jnp.split(mod, 3, axis=-1)
        xm = rms_norm(x, p['norm_g']) * (1 + scale[:, None]) + shift[:, None]
        out, s_fin, h_fin, v = mixer(xm, p, shift_fn, rwkv_init[l], lru_init[l], v_first)
        if v_first is None:
            v_first = v
        x = x + gate[:, None] * out
        s_out.append(s_fin)
        h_out.append(h_fin)
    return x, jnp.stack(s_out, axis=1), jnp.stack(h_out, axis=1)


def setup_inputs(seed: int = 0) -> dict:
    key = jax.random.key(seed)
    ks = iter(jax.random.split(key, 48))
    L, D = DEPTH, D_MODEL

    def nrm(shape, s):
        return jax.random.normal(next(ks), shape, jnp.float32) * s

    def uni(shape, lo, hi):
        return jax.random.uniform(next(ks), shape, jnp.float32, lo, hi)

    inp = {}
    inp['x_prompt'] = nrm((BATCH, SEQ, D), 1.0)
    inp['x_sample'] = nrm((DEC_BATCH, DEC_SEQ, D), 1.0)
    inp['c'] = nrm((DEC_BATCH, D), 1.0)
    inp['state_rwkv'] = nrm((DEC_BATCH, L, 2, H_A, HEAD_A, HEAD_A), 0.5)
    inp['state_lru'] = nrm((DEC_BATCH, L, 2, W_B), 0.5)
    inp['c_ctx'] = nrm((D,), 1.0)
    inp['norm_g'] = 1.0 + nrm((L, D), 0.1)
    inp['ada_w'] = nrm((L, D, 3 * D), 0.5 * D ** -0.5)
    inp['ada_b'] = nrm((L, 3 * D), 0.02)
    inp['w_in'] = nrm((L, D, N_IN), D ** -0.5)
    inp['mu_rkv'] = uni((L, 3, W_A), 0.0, 1.0)
    inp['dec_w0'] = uni((L, 2, W_A), -3.0, 1.0)
    inp['dec_w1'] = nrm((L, 2, D, LORA_DEC), D ** -0.5)
    inp['dec_w2'] = nrm((L, 2, LORA_DEC, W_A), 0.5 * LORA_DEC ** -0.5)
    inp['iclr_w0'] = nrm((L, 2, W_A), 0.5)
    inp['iclr_w1'] = nrm((L, 2, D, LORA_ICLR), D ** -0.5)
    inp['iclr_w2'] = nrm((L, 2, LORA_ICLR, W_A), 0.5 * LORA_ICLR ** -0.5)
    inp['vres_w0'] = nrm((L - 1, W_A), 0.5)
    inp['vres_w1'] = nrm((L - 1, D, LORA_VRES), D ** -0.5)
    inp['vres_w2'] = nrm((L - 1, LORA_VRES, W_A), 0.5 * LORA_VRES ** -0.5)
    inp['k_k'] = 0.85 + nrm((L, W_A), 0.05)
    inp['k_a'] = 1.0 + nrm((L, W_A), 0.05)
    inp['r_k'] = nrm((L, H_A, HEAD_A), 0.1)
    inp['lnx_g'] = 1.0 + nrm((L, W_A), 0.1)
    inp['lnx_b'] = nrm((L, W_A), 0.02)
    inp['w_out_a'] = nrm((L, W_A, D), W_A ** -0.5)
    inp['conv_w'] = nrm((L, CONV_W, W_B), CONV_W ** -0.5)
    inp['conv_b'] = nrm((L, W_B), 0.02)
    inp['gr_w'] = nrm((L, 2, H_B, BLK_B, BLK_B), BLK_B ** -0.5)
    inp['gr_b'] = nrm((L, 2, W_B), 0.02)
    inp['gi_w'] = nrm((L, 2, H_B, BLK_B, BLK_B), BLK_B ** -0.5)
    inp['gi_b'] = nrm((L, 2, W_B), 0.02)
    a_root = uni((L, 2, W_B), 0.9, 0.999) ** (1.0 / LRU_C)
    inp['lru_lambda'] = jnp.log(a_root) - jnp.log1p(-a_root)
    inp['w_out_b'] = nrm((L, W_B, D), W_B ** -0.5)
    inp['w_out'] = nrm((L, D, D), D ** -0.5)
    inp['final_g'] = 1.0 + nrm((D,), 0.1)
    return inp


def reference(x_prompt, x_sample, c, state_rwkv, state_lru, c_ctx, norm_g, ada_w, ada_b, w_in, mu_rkv,
              dec_w0, dec_w1, dec_w2, iclr_w0, iclr_w1, iclr_w2, vres_w0, vres_w1, vres_w2, k_k, k_a, r_k,
              lnx_g, lnx_b, w_out_a, conv_w, conv_b, gr_w, gr_b, gi_w, gi_b, lru_lambda, w_out_b, w_out,
              final_g):
    layers = []
    for l in range(DEPTH):
        p = dict(norm_g=norm_g[l], ada_w=ada_w[l], ada_b=ada_b[l], w_in=w_in[l], mu_rkv=mu_rkv[l],
                 dec_w0=dec_w0[l], dec_w1=dec_w1[l], dec_w2=dec_w2[l], iclr_w0=iclr_w0[l],
                 iclr_w1=iclr_w1[l], iclr_w2=iclr_w2[l], k_k=k_k[l], k_a=k_a[l], r_k=r_k[l],
                 lnx_g=lnx_g[l], lnx_b=lnx_b[l], w_out_a=w_out_a[l], conv_w=conv_w[l], conv_b=conv_b[l],
                 gr_w=gr_w[l], gr_b=gr_b[l], gi_w=gi_w[l], gi_b=gi_b[l], lru_lambda=lru_lambda[l],
                 w_out_b=w_out_b[l], w_out=w_out[l])
        if l > 0:
            p['vres_w0'] = vres_w0[l - 1]
            p['vres_w1'] = vres_w1[l - 1]
            p['vres_w2'] = vres_w2[l - 1]
        layers.append(p)

    bp = x_prompt.shape[0]
    cond_ctx = jnp.broadcast_to(c_ctx, (bp, D_MODEL))
    zeros_rwkv = [jnp.zeros((bp, 2, H_A, HEAD_A, HEAD_A), jnp.float32) for _ in range(DEPTH)]
    zeros_lru = [jnp.zeros((bp, 2, W_B), jnp.float32) for _ in range(DEPTH)]
    x_ctx, new_state_rwkv, new_state_lru = trunk(x_prompt, cond_ctx, shift_1d, zeros_rwkv, zeros_lru, layers)

    rwkv_init = [state_rwkv[:, l] for l in range(DEPTH)]
    lru_init = [state_lru[:, l] for l in range(DEPTH)]
    x_lat, _, _ = trunk(x_sample, c, shift_grid, rwkv_init, lru_init, layers)

    y_prompt = rms_norm(x_ctx, final_g)
    y_sample = rms_norm(x_lat, final_g)
    return (y_prompt, y_sample, new_state_rwkv.astype(x_prompt.dtype), new_state_lru.astype(x_prompt.dtype))
```

```python
import functools
import math

import jax
import jax.numpy as jnp
from jax import lax
from jax.experimental import pallas as pl
from jax.experimental.pallas import tpu as pltpu

F32 = jnp.float32
BF16 = jnp.bfloat16

LANES = 128
SUBLANES = 8
HEAD = 64
CHAIN_BATCH = 8
LORA_COLS = 512
RMS_EPS = 1e-6
GN_EPS = 64e-5
LRU_C = 8.0
KK_EPS = 1e-12
VMEM_LIMIT = 56 * 1024 * 1024

TM_IN = 1024
TN_IN = 512
TM_OUT = 256
TT_PREP = 512
TB_SCAN = 32
ROWS_SCAN = 32
TB_GN = 32
CB_LRU = 256
CH_LRU = 512


def _params(*sem):
    return pltpu.CompilerParams(dimension_semantics=sem, vmem_limit_bytes=VMEM_LIMIT)


def _softplus(x):
    return jnp.maximum(x, 0.0) + jnp.log1p(jnp.exp(-jnp.abs(x)))


def _silu(x):
    return x * jax.nn.sigmoid(x)


def _ada_kernel(c_ref, w_ref, b_ref, o_ref):
    h = _silu(c_ref[...]).astype(BF16)
    o_ref[0] = jnp.dot(h, w_ref[0].astype(BF16), preferred_element_type=F32) + b_ref[0]


def _ada_mod(cond, ada_w, ada_b):
    n_l, d, n = ada_w.shape
    bc = cond.shape[0]
    tn = math.gcd(n, 512)
    return pl.pallas_call(
        _ada_kernel,
        grid=(n_l, n // tn),
        in_specs=[
            pl.BlockSpec((bc, d), lambda l, j: (0, 0)),
            pl.BlockSpec((1, d, tn), lambda l, j: (l, 0, j)),
            pl.BlockSpec((1, 1, tn), lambda l, j: (l, 0, j)),
        ],
        out_specs=pl.BlockSpec((1, bc, tn), lambda l, j: (l, 0, j)),
        out_shape=jax.ShapeDtypeStruct((n_l, bc, n), F32),
        compiler_params=_params("arbitrary", "arbitrary"),
        name="ada",
    )(cond, ada_w, ada_b.reshape(n_l, 1, n))


def _inproj_kernel(x_ref, mod_ref, g_ref, w_ref, o_ref, xm_ref):
    @pl.when(pl.program_id(1) == 0)
    def _():
        x = x_ref[...]
        y = x * lax.rsqrt(jnp.mean(x * x, axis=-1, keepdims=True) + RMS_EPS) * g_ref[...]
        xm_ref[...] = (y * (1.0 + mod_ref[0, 1:2, :]) + mod_ref[0, 0:1, :]).astype(BF16)

    o_ref[...] = jnp.dot(xm_ref[...], w_ref[...], preferred_element_type=F32)


def _inproj(x2, mod, norm_g, w_ext, tm, mod_row):
    m, d = x2.shape
    n = w_ext.shape[1]
    return pl.pallas_call(
        _inproj_kernel,
        grid=(m // tm, n // TN_IN),
        in_specs=[
            pl.BlockSpec((tm, d), lambda i, j: (i, 0)),
            pl.BlockSpec((1, 3, d), lambda i, j: (mod_row(i), 0, 0)),
            pl.BlockSpec((1, d), lambda i, j: (0, 0)),
            pl.BlockSpec((d, TN_IN), lambda i, j: (0, j)),
        ],
        out_specs=pl.BlockSpec((tm, TN_IN), lambda i, j: (i, j)),
        out_shape=jax.ShapeDtypeStruct((m, n), F32),
        scratch_shapes=[pltpu.VMEM((tm, d), BF16)],
        compiler_params=_params("arbitrary", "arbitrary"),
        name="inproj",
    )(x2, mod, norm_g.reshape(1, d), w_ext)


def _prep_kernel(*refs, tt, grid_w, n_t, has_vres):
    it = iter(refs)
    zr, zk, zv = next(it), next(it), next(it)
    halos = [next(it) for _ in range(6)] if grid_w else [None] * 6
    lo_ref = next(it)
    vfirst = next(it) if has_vres else None
    mu, dw0, dw2, iw0, iw2 = (next(it) for _ in range(5))
    vw0, vw2 = (next(it), next(it)) if has_vres else (None, None)
    r_o, k_o, v_o, wf_o, wb_o, af_o, ab_o = (next(it) for _ in range(7))

    row = lax.broadcasted_iota(jnp.int32, (tt, LANES), 0)

    def shift(main, prev, nxt):
        if grid_w:
            i = pl.program_id(1)
            prev = jnp.where(i > 0, prev[0], 0.0)
            nxt = jnp.where(i < n_t - 1, nxt[0], 0.0)
            ext = jnp.concatenate([prev, main, nxt], axis=0)
            up = ext[0:tt]
            down = ext[2 * grid_w:tt + 2 * grid_w]
            col = lax.rem(row, grid_w)
            left = jnp.where(col == 0, 0.0, pltpu.roll(main, 1, 0))
            right = jnp.where(col == grid_w - 1, 0.0, pltpu.roll(main, tt - 1, 0))
            return 0.25 * (up + down + left + right)
        prv = jnp.where(row == 0, 0.0, pltpu.roll(main, 1, 0))
        nx = jnp.where(row == tt - 1, 0.0, pltpu.roll(main, tt - 1, 0))
        return 0.5 * (prv + nx)

    def mix(z_ref, hp, hn, j):
        z = z_ref[0]
        return z + (shift(z, hp, hn) - z) * mu[j:j + 1, :]

    r_o[0] = mix(zr, halos[0], halos[1], 0)
    k_o[0] = mix(zk, halos[2], halos[3], 1)
    v = mix(zv, halos[4], halos[5], 2)

    lo = lo_ref[0]
    if has_vres:
        gate = jax.nn.sigmoid(
            vw0[...] + jnp.dot(lo[:, 384:512].astype(BF16), vw2[...], preferred_element_type=F32))
        v = v + (vfirst[0] - v) * gate
    v_o[0] = v

    td = jnp.tanh(lo[:, 0:256]).astype(BF16)
    li = lo[:, 128:384].astype(BF16)
    for d, (w_o, a_o) in enumerate(((wf_o, af_o), (wb_o, ab_o))):
        xd = dw0[d:d + 1, :] + jnp.dot(td, dw2[d], preferred_element_type=F32)
        w_log = -_softplus(-xd) - 0.5
        w_o[0] = jnp.exp(-jnp.exp(w_log))
        a_o[0] = jax.nn.sigmoid(iw0[d:d + 1, :] + jnp.dot(li, iw2[d], preferred_element_type=F32))


def _prep(z3, v_first, lw, grid_w, w_a, n_main):
    b, t, _ = z3.shape
    nc = w_a // LANES
    has_vres = v_first is not None
    if grid_w:
        tt = min(TT_PREP, t)
        assert tt % grid_w == 0 and grid_w % SUBLANES == 0 and t % tt == 0
    else:
        tt = t
    n_t = t // tt
    blk = lambda off: pl.BlockSpec((1, tt, LANES), lambda bb, i, c, off=off: (bb, i, off + c))
    in_specs = [blk(0), blk(nc), blk(2 * nc)]
    args = [z3, z3, z3]
    if grid_w:
        per, last = tt // grid_w, t // grid_w - 1
        for off in (0, nc, 2 * nc):
            in_specs.append(pl.BlockSpec(
                (1, grid_w, LANES),
                lambda bb, i, c, off=off: (bb, jnp.maximum(i * per - 1, 0), off + c)))
            in_specs.append(pl.BlockSpec(
                (1, grid_w, LANES),
                lambda bb, i, c, off=off: (bb, jnp.minimum((i + 1) * per, last), off + c)))
            args += [z3, z3]
    in_specs.append(pl.BlockSpec((1, tt, LORA_COLS), lambda bb, i, c: (bb, i, n_main // LORA_COLS)))
    args.append(z3)
    if has_vres:
        in_specs.append(blk(0))
        args.append(v_first)
    wcol = lambda shape: pl.BlockSpec(shape, lambda bb, i, c: (0,) * (len(shape) - 1) + (c,))
    in_specs += [wcol((3, LANES)), wcol((2, LANES)), wcol((2, 256, LANES)), wcol((2, LANES)),
                 wcol((2, 256, LANES))]
    args += [lw["mu"], lw["dec_w0"], lw["dec_w2p"], lw["iclr_w0"], lw["iclr_w2p"]]
    if has_vres:
        in_specs += [wcol((1, LANES)), wcol((128, LANES))]
        args += [lw["vres_w0"], lw["vres_w2p"]]
    out = jax.ShapeDtypeStruct((b, t, w_a), F32)
    return pl.pallas_call(
        functools.partial(_prep_kernel, tt=tt, grid_w=grid_w, n_t=n_t, has_vres=has_vres),
        grid=(b, n_t, nc),
        in_specs=in_specs,
        out_specs=[blk(0)] * 7,
        out_shape=[out] * 7,
        compiler_params=_params("arbitrary", "arbitrary", "arbitrary"),
        name="prep",
    )(*args)


def _scan_kernel(r_ref, k_ref, v_ref, w_ref, a_ref, kkc_ref, kac_ref, s0_ref, y_ref, sfin_ref,
                 s_scr, kkn, kd, ka, *, tb, reverse):
    i = pl.program_id(1)

    @pl.when(i == 0)
    def _():
        s_scr[...] = s0_ref[0]

    k = k_ref[0]
    a = a_ref[0]
    kkr = k * kkc_ref[...][None]
    nrm = jnp.sqrt(jnp.sum(kkr * kkr, axis=1, keepdims=True))
    kk = kkr / jnp.maximum(nrm, KK_EPS)
    kkn[...] = kk
    kd[...] = k * (1.0 + (a - 1.0) * kac_ref[...][None])
    ka[...] = kk * a

    def step(s, carry):
        t = (tb - 1 - s) if reverse else s
        for g in range(HEAD // ROWS_SCAN):
            rows = pl.ds(g * ROWS_SCAN, ROWS_SCAN)
            acc = [None, None]
            for kx in range(HEAD):
                term = s_scr[kx, rows, :] * kkn[t, pl.ds(kx, 1), :]
                acc[kx % 2] = term if acc[kx % 2] is None else acc[kx % 2] + term
            skk = acc[0] + acc[1]
            vv = v_ref[0, t, rows, :]
            yac = [None, None]
            for kx in range(HEAD):
                kr = pl.ds(kx, 1)
                sn = (s_scr[kx, rows, :] * w_ref[0, t, kr, :] - skk * ka[t, kr, :]
                      + vv * kd[t, kr, :])
                s_scr[kx, rows, :] = sn
                term = sn * r_ref[0, t, kr, :]
                yac[kx % 2] = term if yac[kx % 2] is None else yac[kx % 2] + term
            y_ref[0, t, rows, :] = yac[0] + yac[1]
        return carry

    lax.fori_loop(0, tb, step, 0)

    @pl.when(i == pl.num_programs(1) - 1)
    def _():
        sfin_ref[0] = s_scr[...]


def _scan(r, k, v, w, a, kk_c, ka_c, s0, reverse):
    nt, t = r.shape[0], r.shape[1]
    tb = min(TB_SCAN, t)
    nb = t // tb
    tmap = (lambda c, i: (c, nb - 1 - i, 0, 0)) if reverse else (lambda c, i: (c, i, 0, 0))
    seq = pl.BlockSpec((1, tb, HEAD, LANES), tmap)
    cst = pl.BlockSpec((HEAD, LANES), lambda c, i: (0, 0))
    st = pl.BlockSpec((1, HEAD, HEAD, LANES), lambda c, i: (c, 0, 0, 0))
    return pl.pallas_call(
        functools.partial(_scan_kernel, tb=tb, reverse=reverse),
        grid=(nt, nb),
        in_specs=[seq] * 5 + [cst, cst, st],
        out_specs=[seq, st],
        out_shape=[jax.ShapeDtypeStruct(r.shape, F32), jax.ShapeDtypeStruct(s0.shape, F32)],
        scratch_shapes=[pltpu.VMEM((HEAD, HEAD, LANES), F32)] + [pltpu.VMEM((tb, HEAD, LANES), F32)] * 3,
        compiler_params=_params("arbitrary", "arbitrary"),
        name="scan_bwd" if reverse else "scan_fwd",
    )(r, k, v, w, a, kk_c, ka_c, s0)


def _gn_kernel(yf_ref, yb_ref, r_ref, k_ref, v_ref, af_ref, ab_ref, kac_ref, rkc_ref, g_ref, b_ref, o_ref):
    y = yf_ref[0] + yb_ref[0]
    mean = jnp.mean(y, axis=1, keepdims=True)
    dlt = y - mean
    var = jnp.mean(dlt * dlt, axis=1, keepdims=True)
    yn = dlt * lax.rsqrt(var + GN_EPS) * g_ref[...][None] + b_ref[...][None]
    r, k = r_ref[0], k_ref[0]
    kac = kac_ref[...][None]
    rk = r * rkc_ref[...][None]
    bo_f = jnp.sum(rk * (k * (1.0 + (af_ref[0] - 1.0) * kac)), axis=1, keepdims=True)
    bo_b = jnp.sum(rk * (k * (1.0 + (ab_ref[0] - 1.0) * kac)), axis=1, keepdims=True)
    o_ref[0] = yn + (bo_f + bo_b) * v_ref[0]


def _gn(yf, yb, r, k, v, af, ab, ka_c, rk_c, g_c, b_c):
    nt, t = r.shape[0], r.shape[1]
    tb = min(TB_GN, t)
    seq = pl.BlockSpec((1, tb, HEAD, LANES), lambda c, i: (c, i, 0, 0))
    cst = pl.BlockSpec((HEAD, LANES), lambda c, i: (0, 0))
    return pl.pallas_call(
        _gn_kernel,
        grid=(nt, t // tb),
        in_specs=[seq] * 7 + [cst] * 4,
        out_specs=seq,
        out_shape=jax.ShapeDtypeStruct(r.shape, F32),
        compiler_params=_params("arbitrary", "arbitrary"),
        name="gn",
    )(yf, yb, r, k, v, af, ab, ka_c, rk_c, g_c, b_c)


def _lru_kernel(x_ref, h0_ref, cw_ref, cb_ref, wr_ref, br_ref, wi_ref, bi_ref, lam_ref, o_ref, hfin_ref,
                xpad, af, uf, ab, ub, *, t_len, ch):
    cbk = x_ref.shape[-1]
    pad = SUBLANES
    xpad[0:pad, :] = jnp.zeros((pad, cbk), F32)
    xpad[t_len + pad:t_len + 2 * pad, :] = jnp.zeros((pad, cbk), F32)
    xpad[pad:t_len + pad, :] = x_ref[0]
    cw = cw_ref[...]
    for c0 in range(0, t_len, ch):
        xc = cb_ref[...] + xpad[c0 + pad - 2:c0 + pad - 2 + ch, :] * cw[0:1, :]
        for j in range(1, 4):
            xc = xc + xpad[c0 + pad - 2 + j:c0 + pad - 2 + j + ch, :] * cw[j:j + 1, :]
        xcb = xc.astype(BF16)
        for d, (a_s, u_s) in enumerate(((af, uf), (ab, ub))):
            rg = jax.nn.sigmoid(jnp.dot(xcb, wr_ref[d, 0], preferred_element_type=F32) + br_ref[d:d + 1, :])
            ig = jax.nn.sigmoid(jnp.dot(xcb, wi_ref[d, 0], preferred_element_type=F32) + bi_ref[d:d + 1, :])
            log_a = (-LRU_C * _softplus(-lam_ref[d:d + 1, :])) * rg
            a_s[c0:c0 + ch, :] = jnp.exp(log_a)
            u_s[c0:c0 + ch, :] = jnp.sqrt(1.0 - jnp.exp(2.0 * log_a)) * (ig * xc)

    rowi = lax.broadcasted_iota(jnp.int32, (SUBLANES, cbk), 0)

    def tile_scan(a, u, h_in, backward):
        for s in (1, 2, 4):
            sh = SUBLANES - s if backward else s
            a_s, u_s = pltpu.roll(a, sh, 0), pltpu.roll(u, sh, 0)
            m = (rowi < SUBLANES - s) if backward else (rowi >= s)
            u = jnp.where(m, u + a * u_s, u)
            a = jnp.where(m, a * a_s, a)
        h = a * h_in + u
        return h, (h[0:1, :] if backward else h[SUBLANES - 1:SUBLANES, :])

    def body(i, carry):
        hf, hb = carry
        of = pl.multiple_of(i * SUBLANES, SUBLANES)
        h, hf = tile_scan(af[pl.ds(of, SUBLANES), :], uf[pl.ds(of, SUBLANES), :], hf, False)
        af[pl.ds(of, SUBLANES), :] = h
        ob = pl.multiple_of(t_len - SUBLANES - i * SUBLANES, SUBLANES)
        h, hb = tile_scan(ab[pl.ds(ob, SUBLANES), :], ub[pl.ds(ob, SUBLANES), :], hb, True)
        ab[pl.ds(ob, SUBLANES), :] = h
        return hf, hb

    hf, hb = lax.fori_loop(0, t_len // SUBLANES, body, (h0_ref[0, 0:1, :], h0_ref[0, 1:2, :]))
    o_ref[0] = af[...] + ab[...]
    hfin_ref[0, 0:1, :] = hf
    hfin_ref[0, 1:2, :] = hb


def _lru(z3, h0, lw, w_a, w_b):
    b, t, _ = z3.shape
    cbk = CB_LRU
    ncb = w_b // cbk
    off = 4 * w_a // cbk
    ch = min(CH_LRU, t)
    wcol = lambda rows: pl.BlockSpec((rows, cbk), lambda bb, c: (0, c))
    wgate = pl.BlockSpec((2, 1, cbk, cbk), lambda bb, c: (0, c, 0, 0))
    st = pl.BlockSpec((1, 2, cbk), lambda bb, c: (bb, 0, c))
    return pl.pallas_call(
        functools.partial(_lru_kernel, t_len=t, ch=ch),
        grid=(b, ncb),
        in_specs=[pl.BlockSpec((1, t, cbk), lambda bb, c: (bb, 0, off + c)), st,
                  wcol(4), wcol(1), wgate, wcol(2), wgate, wcol(2), wcol(2)],
        out_specs=[pl.BlockSpec((1, t, cbk), lambda bb, c: (bb, 0, c)), st],
        out_shape=[jax.ShapeDtypeStruct((b, t, w_b), F32), jax.ShapeDtypeStruct((b, 2, w_b), F32)],
        scratch_shapes=[pltpu.VMEM((t + 2 * SUBLANES, cbk), F32)] + [pltpu.VMEM((t, cbk), F32)] * 4,
        compiler_params=_params("arbitrary", "arbitrary"),
        name="lru",
    )(z3, h0, lw["conv_w"], lw["conv_b"], lw["gr_bd"], lw["gr_b"], lw["gi_bd"], lw["gi_b"], lw["lam"])


def _out_kernel(y_ref, ga_ref, h_ref, gb_ref, ma_ref, mb_ref, x_ref, mod_ref, woa_ref, wob_ref, wo_ref,
                fg_ref, o_ref, *, final):
    ya = jnp.dot((y_ref[...] * _silu(ga_ref[...])).astype(BF16), woa_ref[...], preferred_element_type=F32)
    yb = jnp.dot((h_ref[...] * _silu(gb_ref[...])).astype(BF16), wob_ref[...], preferred_element_type=F32)
    merged = jax.nn.sigmoid(ma_ref[...]) * ya + jax.nn.sigmoid(mb_ref[...]) * yb
    out = jnp.dot(merged.astype(BF16), wo_ref[...], preferred_element_type=F32)
    xn = x_ref[...] + mod_ref[0, 2:3, :] * out
    if final:
        xn = xn * lax.rsqrt(jnp.mean(xn * xn, axis=-1, keepdims=True) + RMS_EPS) * fg_ref[...]
    o_ref[...] = xn


def _out(y2, z2, h2, x2, mod, lw, final_g, tm, mod_row, final, w_a, w_b):
    m, d = x2.shape
    once = dict(pipeline_mode=pl.Buffered(1))
    zoff_ga = 3 * w_a // w_a
    zoff_gb = (4 * w_a + w_b) // w_b
    zoff_ma = (4 * w_a + 2 * w_b) // d
    return pl.pallas_call(
        functools.partial(_out_kernel, final=final),
        grid=(m // tm,),
        in_specs=[
            pl.BlockSpec((tm, w_a), lambda i: (i, 0)),
            pl.BlockSpec((tm, w_a), lambda i: (i, zoff_ga)),
            pl.BlockSpec((tm, w_b), lambda i: (i, 0)),
            pl.BlockSpec((tm, w_b), lambda i: (i, zoff_gb)),
            pl.BlockSpec((tm, d), lambda i: (i, zoff_ma)),
            pl.BlockSpec((tm, d), lambda i: (i, zoff_ma + 1)),
            pl.BlockSpec((tm, d), lambda i: (i, 0)),
            pl.BlockSpec((1, 3, d), lambda i: (mod_row(i), 0, 0)),
            pl.BlockSpec((w_a, d), lambda i: (0, 0), **once),
            pl.BlockSpec((w_b, d), lambda i: (0, 0), **once),
            pl.BlockSpec((d, d), lambda i: (0, 0), **once),
            pl.BlockSpec((1, d), lambda i: (0, 0)),
        ],
        out_specs=pl.BlockSpec((tm, d), lambda i: (i, 0)),
        out_shape=jax.ShapeDtypeStruct((m, d), F32),
        compiler_params=_params("arbitrary"),
        name="out_final" if final else "out",
    )(y2, z2, h2, z2, z2, z2, x2, mod, lw["w_out_a"], lw["w_out_b"], lw["w_out"], final_g.reshape(1, d))


def _to_chains(x, heads):
    b, t, _ = x.shape
    nt = b // CHAIN_BATCH
    x = x.reshape(nt, CHAIN_BATCH, t, heads, HEAD)
    return x.transpose(0, 2, 4, 1, 3).reshape(nt, t, HEAD, CHAIN_BATCH * heads)


def _from_chains(x, heads):
    nt, t = x.shape[0], x.shape[1]
    x = x.reshape(nt, t, HEAD, CHAIN_BATCH, heads)
    return x.transpose(0, 3, 1, 4, 2).reshape(nt * CHAIN_BATCH, t, heads * HEAD)


def _chain_const(p, heads):
    t = p.reshape(heads, HEAD).T
    return jnp.broadcast_to(t[:, None, :], (HEAD, CHAIN_BATCH, heads)).reshape(HEAD, CHAIN_BATCH * heads)


def _state_to_chains(s):
    b, h = s.shape[0], s.shape[1]
    nt = b // CHAIN_BATCH
    s = s.reshape(nt, CHAIN_BATCH, h, HEAD, HEAD)
    return s.transpose(0, 4, 3, 1, 2).reshape(nt, HEAD, HEAD, CHAIN_BATCH * h)


def _state_from_chains(s, heads):
    nt = s.shape[0]
    s = s.reshape(nt, HEAD, HEAD, CHAIN_BATCH, heads)
    return s.transpose(0, 3, 4, 2, 1).reshape(nt * CHAIN_BATCH, heads, HEAD, HEAD)


def _block_diag(w, per):
    h, n, _ = w.shape
    g = h // per
    eye = jnp.eye(per, dtype=w.dtype)
    wg = w.reshape(g, per, n, n)
    return (wg[:, :, :, None, :] * eye[None, :, None, :, None]).reshape(g, per * n, per * n)


def _layer_weights(l, n_layers, heads, w_in, mu_rkv, dec_w0, dec_w1, dec_w2, iclr_w0, iclr_w1, iclr_w2,
                   vres_w0, vres_w1, vres_w2, k_k, k_a, r_k, lnx_g, lnx_b, w_out_a, conv_w, conv_b, gr_w,
                   gr_b, gi_w, gi_b, lru_lambda, w_out_b, w_out):
    d = w_in.shape[1]
    w_a = heads * HEAD
    r_dec, r_iclr, r_vres = dec_w1.shape[-1], iclr_w1.shape[-1], vres_w1.shape[-1]
    assert (r_dec, r_iclr, r_vres) == (96, 96, 64), "LoRA column layout assumes ranks 96/96/64"
    has_vres = l > 0
    vw1 = vres_w1[l - 1] if has_vres else jnp.zeros((d, r_vres), F32)
    used = 2 * r_dec + 2 * r_iclr + r_vres
    w_ext = jnp.concatenate(
        [w_in[l], dec_w1[l, 0], dec_w1[l, 1], iclr_w1[l, 0], iclr_w1[l, 1], vw1,
         jnp.zeros((d, LORA_COLS - used), F32)], axis=1).astype(BF16)
    dec_w2p = jnp.zeros((2, 256, w_a), F32)
    iclr_w2p = jnp.zeros((2, 256, w_a), F32)
    for dd in range(2):
        dec_w2p = dec_w2p.at[dd, r_dec * dd:r_dec * (dd + 1)].set(dec_w2[l, dd])
        o = 2 * r_dec - 128 + r_iclr * dd
        iclr_w2p = iclr_w2p.at[dd, o:o + r_iclr].set(iclr_w2[l, dd])
    lw = dict(
        w_ext=w_ext, mu=mu_rkv[l], dec_w0=dec_w0[l], dec_w2p=dec_w2p.astype(BF16), iclr_w0=iclr_w0[l],
        iclr_w2p=iclr_w2p.astype(BF16),
        kk_c=_chain_const(k_k[l], heads), ka_c=_chain_const(k_a[l], heads),
        rk_c=_chain_const(r_k[l].reshape(-1), heads), g_c=_chain_const(lnx_g[l], heads),
        b_c=_chain_const(lnx_b[l], heads),
        conv_w=conv_w[l], conv_b=conv_b[l].reshape(1, -1),
        gr_bd=jnp.stack([_block_diag(gr_w[l, dd], CB_LRU // gr_w.shape[-1]) for dd in range(2)]).astype(BF16),
        gi_bd=jnp.stack([_block_diag(gi_w[l, dd], CB_LRU // gi_w.shape[-1]) for dd in range(2)]).astype(BF16),
        gr_b=gr_b[l], gi_b=gi_b[l], lam=lru_lambda[l],
        w_out_a=w_out_a[l].astype(BF16), w_out_b=w_out_b[l].astype(BF16), w_out=w_out[l].astype(BF16),
    )
    if has_vres:
        vres_w2p = jnp.zeros((128, w_a), F32).at[0:r_vres].set(vres_w2[l - 1])
        lw.update(vres_w0=vres_w0[l - 1].reshape(1, -1), vres_w2p=vres_w2p.astype(BF16))
    return lw


def _trunk(x, mods, mod_row_in, mod_row_out, tm_in, tm_out, grid_w, rwkv_init, lru_init, layers, final_g,
           heads, w_b):
    b, t, d = x.shape
    w_a = heads * HEAD
    n_layers = len(layers)
    x2 = x.reshape(b * t, d)
    v_first = None
    s_out, h_out = [], []
    for l, lw in enumerate(layers):
        n_ext = lw["w_ext"].shape[1]
        z2 = _inproj(x2, mods[l], lw["norm_g"], lw["w_ext"], tm_in, mod_row_in)
        z3 = z2.reshape(b, t, n_ext)
        r, k, v, w_f, w_bk, a_f, a_bk = _prep(z3, v_first, lw, grid_w, w_a, n_ext - LORA_COLS)
        if v_first is None:
            v_first = v
        rc, kc, vc = (_to_chains(q, heads) for q in (r, k, v))
        afc, abc = _to_chains(a_f, heads), _to_chains(a_bk, heads)
        y_f, s_f = _scan(rc, kc, vc, _to_chains(w_f, heads), afc, lw["kk_c"], lw["ka_c"], rwkv_init[l][0], False)
        y_b, s_b = _scan(rc, kc, vc, _to_chains(w_bk, heads), abc, lw["kk_c"], lw["ka_c"], rwkv_init[l][1], True)
        y_rwkv = _from_chains(
            _gn(y_f, y_b, rc, kc, vc, afc, abc, lw["ka_c"], lw["rk_c"], lw["g_c"], lw["b_c"]), heads)
        h_sum, h_fin = _lru(z3, lru_init[l], lw, w_a, w_b)
        x2 = _out(y_rwkv.reshape(b * t, w_a), z2, h_sum.reshape(b * t, w_b), x2, mods[l], lw, final_g,
                  tm_out, mod_row_out, l == n_layers - 1, w_a, w_b)
        s_out.append((s_f, s_b))
        h_out.append(h_fin)
    return x2.reshape(b, t, d), s_out, h_out


def kernel(x_prompt, x_sample, c, state_rwkv, state_lru, c_ctx, norm_g, ada_w, ada_b, w_in, mu_rkv, dec_w0,
           dec_w1, dec_w2, iclr_w0, iclr_w1, iclr_w2, vres_w0, vres_w1, vres_w2, k_k, k_a, r_k, lnx_g, lnx_b,
           w_out_a, conv_w, conv_b, gr_w, gr_b, gi_w, gi_b, lru_lambda, w_out_b, w_out, final_g):
    n_layers, d = norm_g.shape
    bp, tp, _ = x_prompt.shape
    bs, ts, _ = x_sample.shape
    heads = r_k.shape[1]
    w_b = conv_w.shape[-1]
    grid_w = int(round(ts ** 0.5))
    assert grid_w * grid_w == ts and heads * CHAIN_BATCH == LANES
    assert bp % CHAIN_BATCH == 0 and bs % CHAIN_BATCH == 0

    layers = []
    for l in range(n_layers):
        lw = _layer_weights(l, n_layers, heads, w_in, mu_rkv, dec_w0, dec_w1, dec_w2, iclr_w0, iclr_w1,
                            iclr_w2, vres_w0, vres_w1, vres_w2, k_k, k_a, r_k, lnx_g, lnx_b, w_out_a, conv_w,
                            conv_b, gr_w, gr_b, gi_w, gi_b, lru_lambda, w_out_b, w_out)
        lw["norm_g"] = norm_g[l]
        layers.append(lw)

    ctx_row = bs
    n_rows = -(-(bs + 1) // SUBLANES) * SUBLANES
    cond = jnp.zeros((n_rows, d), F32).at[0:bs].set(c).at[ctx_row].set(c_ctx)
    mod = _ada_mod(cond, ada_w, ada_b).reshape(n_layers, n_rows, 3, d)
    mods = [mod[l] for l in range(n_layers)]

    zero_s = jnp.zeros((bp // CHAIN_BATCH, HEAD, HEAD, LANES), F32)
    tm_p = min(TM_IN, bp * tp)
    tm_po = min(TM_OUT, bp * tp)
    x_ctx, s_ctx, h_ctx = _trunk(
        x_prompt, mods, lambda i: ctx_row, lambda i: ctx_row, tm_p, tm_po, 0,
        [(zero_s, zero_s)] * n_layers, [jnp.zeros((bp, 2, w_b), F32)] * n_layers, layers, final_g, heads, w_b)

    tm_s = min(TM_IN, ts)
    tm_so = min(TM_OUT, ts)
    assert ts % tm_s == 0 and ts % tm_so == 0
    rwkv_init = [tuple(_state_to_chains(state_rwkv[:, l, dd]) for dd in range(2)) for l in range(n_layers)]
    x_lat, _, _ = _trunk(
        x_sample, mods, lambda i: i // (ts // tm_s), lambda i: i // (ts // tm_so), tm_s, tm_so, grid_w,
        rwkv_init, [state_lru[:, l] for l in range(n_layers)], layers, final_g, heads, w_b)

    new_rwkv = jnp.stack(
        [jnp.stack([_state_from_chains(s_ctx[l][dd], heads) for dd in range(2)], axis=1)
         for l in range(n_layers)], axis=1)
    new_lru = jnp.stack(h_ctx, axis=1)
    return x_ctx, x_lat, new_rwkv, new_lru
```

```python
import functools
import math

import jax
import jax.numpy as jnp
from jax import lax
from jax.experimental import pallas as pl
from jax.experimental.pallas import tpu as pltpu

F32 = jnp.float32
BF16 = jnp.bfloat16

LANES = 128
SUBLANES = 8
HEAD = 64
CHAIN_BATCH = 8
LORA_COLS = 512
RMS_EPS = 1e-6
GN_EPS = 64e-5
LRU_C = 8.0
KK_EPS = 1e-12
DECAY_SCALE = math.exp(-0.5)
VMEM_LIMIT = 56 * 1024 * 1024

TM_IN = 1024
TN_IN = 512
TM_OUT = 256
TT_PREP = 512
ROWS_PREP = 128
TB_SCAN = 32
ROWS_SCAN = 32
TB_GN = 32
CB_LRU = 256
CH_LRU = 512


def _params(*sem):
    return pltpu.CompilerParams(dimension_semantics=sem, vmem_limit_bytes=VMEM_LIMIT)


def _softplus(x):
    return jnp.maximum(x, 0.0) + jnp.log1p(jnp.exp(-jnp.abs(x)))


def _sigmoid(x):
    return 0.5 * jnp.tanh(0.5 * x) + 0.5


def _silu(x):
    return x * _sigmoid(x)


def _ada_kernel(c_ref, w_ref, b_ref, o_ref):
    h = _silu(c_ref[...]).astype(BF16)
    o_ref[0] = jnp.dot(h, w_ref[0].astype(BF16), preferred_element_type=F32) + b_ref[0]


def _ada_mod(cond, ada_w, ada_b):
    n_l, d, n = ada_w.shape
    bc = cond.shape[0]
    tn = math.gcd(n, 512)
    return pl.pallas_call(
        _ada_kernel,
        grid=(n_l, n // tn),
        in_specs=[
            pl.BlockSpec((bc, d), lambda l, j: (0, 0)),
            pl.BlockSpec((1, d, tn), lambda l, j: (l, 0, j)),
            pl.BlockSpec((1, 1, tn), lambda l, j: (l, 0, j)),
        ],
        out_specs=pl.BlockSpec((1, bc, tn), lambda l, j: (l, 0, j)),
        out_shape=jax.ShapeDtypeStruct((n_l, bc, n), F32),
        compiler_params=_params("arbitrary", "arbitrary"),
        name="ada",
    )(cond, ada_w, ada_b.reshape(n_l, 1, n))


def _inproj_kernel(x_ref, mod_ref, g_ref, w_ref, o_ref, xm_ref):
    @pl.when(pl.program_id(1) == 0)
    def _():
        x = x_ref[...]
        y = x * lax.rsqrt(jnp.mean(x * x, axis=-1, keepdims=True) + RMS_EPS) * g_ref[...]
        xm_ref[...] = (y * (1.0 + mod_ref[0, 1:2, :]) + mod_ref[0, 0:1, :]).astype(BF16)

    o_ref[...] = jnp.dot(xm_ref[...], w_ref[...], preferred_element_type=F32)


def _inproj(x2, mod, norm_g, w_ext, tm, mod_row):
    m, d = x2.shape
    n = w_ext.shape[1]
    return pl.pallas_call(
        _inproj_kernel,
        grid=(m // tm, n // TN_IN),
        in_specs=[
            pl.BlockSpec((tm, d), lambda i, j: (i, 0)),
            pl.BlockSpec((1, 3, d), lambda i, j: (mod_row(i), 0, 0)),
            pl.BlockSpec((1, d), lambda i, j: (0, 0)),
            pl.BlockSpec((d, TN_IN), lambda i, j: (0, j)),
        ],
        out_specs=pl.BlockSpec((tm, TN_IN), lambda i, j: (i, j)),
        out_shape=jax.ShapeDtypeStruct((m, n), F32),
        scratch_shapes=[pltpu.VMEM((tm, d), BF16)],
        compiler_params=_params("arbitrary", "arbitrary"),
        name="inproj",
    )(x2, mod, norm_g.reshape(1, d), w_ext)


def _prep_kernel(*refs, tt, grid_w, n_t, has_vres):
    it = iter(refs)
    zr, zk, zv = next(it), next(it), next(it)
    halos = [next(it) for _ in range(6)] if grid_w else [None] * 6
    lo_ref = next(it)
    vfirst = next(it) if has_vres else None
    mu, dw0, dw2, iw0, iw2 = (next(it) for _ in range(5))
    vw0, vw2 = (next(it), next(it)) if has_vres else (None, None)
    r_o, k_o, v_o, wf_o, wb_o, af_o, ab_o = (next(it) for _ in range(7))
    td_s, li_s, lv_s, er_s, ek_s, ev_s = (next(it) for _ in range(6))

    @pl.when(pl.program_id(2) == 0)
    def _():
        td_s[...] = jnp.tanh(lo_ref[0, :, 0:256]).astype(BF16)
        li_s[...] = lo_ref[0, :, 128:384].astype(BF16)
        lv_s[...] = lo_ref[0, :, 384:512].astype(BF16)

    pad = grid_w if grid_w else SUBLANES
    for e_s, z_ref, hp, hn in ((er_s, zr, halos[0], halos[1]), (ek_s, zk, halos[2], halos[3]),
                               (ev_s, zv, halos[4], halos[5])):
        if grid_w:
            i = pl.program_id(1)
            e_s[0:pad, :] = jnp.where(i > 0, hp[0], 0.0)
            e_s[pad + tt:2 * pad + tt, :] = jnp.where(i < n_t - 1, hn[0], 0.0)
        else:
            e_s[0:pad, :] = jnp.zeros((pad, LANES), F32)
            e_s[pad + tt:2 * pad + tt, :] = jnp.zeros((pad, LANES), F32)
        e_s[pad:pad + tt, :] = z_ref[0]

    rc = min(tt, ROWS_PREP)
    col = lax.rem(lax.broadcasted_iota(jnp.int32, (rc, LANES), 0), grid_w) if grid_w else None

    def mix(e_s, c0, j):
        z = e_s[pad + c0:pad + c0 + rc, :]
        before = e_s[pad + c0 - 1:pad + c0 - 1 + rc, :]
        after = e_s[pad + c0 + 1:pad + c0 + 1 + rc, :]
        if grid_w:
            up = e_s[c0:c0 + rc, :]
            down = e_s[2 * pad + c0:2 * pad + c0 + rc, :]
            left = jnp.where(col == 0, 0.0, before)
            right = jnp.where(col == grid_w - 1, 0.0, after)
            sh = 0.25 * (up + down + left + right)
        else:
            sh = 0.5 * (before + after)
        return z + (sh - z) * mu[j:j + 1, :]

    for c0 in range(0, tt, rc):
        rows = pl.ds(c0, rc)
        r_o[0, rows, :] = mix(er_s, c0, 0)
        k_o[0, rows, :] = mix(ek_s, c0, 1)
        v = mix(ev_s, c0, 2)
        if has_vres:
            gate = _sigmoid(vw0[...] + jnp.dot(lv_s[rows, :], vw2[...], preferred_element_type=F32))
            v = v + (vfirst[0, rows, :] - v) * gate
        v_o[0, rows, :] = v
        td = td_s[rows, :]
        li = li_s[rows, :]
        for d, (w_o, a_o) in enumerate(((wf_o, af_o), (wb_o, ab_o))):
            xd = dw0[d:d + 1, :] + jnp.dot(td, dw2[d], preferred_element_type=F32)
            w_o[0, rows, :] = jnp.exp(-DECAY_SCALE * _sigmoid(xd))
            a_o[0, rows, :] = _sigmoid(iw0[d:d + 1, :] + jnp.dot(li, iw2[d], preferred_element_type=F32))


def _prep(z3, v_first, lw, grid_w, w_a, n_main):
    b, t, _ = z3.shape
    nc = w_a // LANES
    has_vres = v_first is not None
    if grid_w:
        tt = min(TT_PREP, t)
        assert min(tt, ROWS_PREP) % grid_w == 0 and grid_w % SUBLANES == 0 and t % tt == 0
    else:
        tt = t
    n_t = t // tt
    blk = lambda off: pl.BlockSpec((1, tt, LANES), lambda bb, i, c, off=off: (bb, i, off + c))
    in_specs = [blk(0), blk(nc), blk(2 * nc)]
    args = [z3, z3, z3]
    if grid_w:
        per, last = tt // grid_w, t // grid_w - 1
        for off in (0, nc, 2 * nc):
            in_specs.append(pl.BlockSpec(
                (1, grid_w, LANES),
                lambda bb, i, c, off=off: (bb, jnp.maximum(i * per - 1, 0), off + c)))
            in_specs.append(pl.BlockSpec(
                (1, grid_w, LANES),
                lambda bb, i, c, off=off: (bb, jnp.minimum((i + 1) * per, last), off + c)))
            args += [z3, z3]
    in_specs.append(pl.BlockSpec((1, tt, LORA_COLS), lambda bb, i, c: (bb, i, n_main // LORA_COLS)))
    args.append(z3)
    if has_vres:
        in_specs.append(blk(0))
        args.append(v_first)
    wcol = lambda shape: pl.BlockSpec(shape, lambda bb, i, c: (0,) * (len(shape) - 1) + (c,))
    in_specs += [wcol((3, LANES)), wcol((2, LANES)), wcol((2, 256, LANES)), wcol((2, LANES)),
                 wcol((2, 256, LANES))]
    args += [lw["mu"], lw["dec_w0"], lw["dec_w2p"], lw["iclr_w0"], lw["iclr_w2p"]]
    if has_vres:
        in_specs += [wcol((1, LANES)), wcol((128, LANES))]
        args += [lw["vres_w0"], lw["vres_w2p"]]
    out = jax.ShapeDtypeStruct((b, t, w_a), F32)
    return pl.pallas_call(
        functools.partial(_prep_kernel, tt=tt, grid_w=grid_w, n_t=n_t, has_vres=has_vres),
        grid=(b, n_t, nc),
        in_specs=in_specs,
        out_specs=[blk(0)] * 7,
        out_shape=[out] * 7,
        scratch_shapes=[pltpu.VMEM((tt, 256), BF16), pltpu.VMEM((tt, 256), BF16), pltpu.VMEM((tt, 128), BF16)]
        + [pltpu.VMEM((tt + 2 * (grid_w if grid_w else SUBLANES), LANES), F32)] * 3,
        compiler_params=_params("arbitrary", "arbitrary", "arbitrary"),
        name="prep",
    )(*args)


def _scan_kernel(r_ref, k_ref, v_ref, w_ref, a_ref, kkc_ref, kac_ref, s0_ref, y_ref, sfin_ref,
                 s_scr, al_s, be_s, kt_s, rt_s, *, tb, reverse):
    i = pl.program_id(1)

    @pl.when(i == 0)
    def _():
        s_scr[...] = s0_ref[0]

    off = 1 if reverse else 0
    al_s[0 if reverse else tb] = jnp.zeros((HEAD, LANES), F32)

    def prep_step(s, wcum):
        t = (tb - 1 - s) if reverse else s
        k_t, a_t = k_ref[0, t], a_ref[0, t]
        kkr = k_t * kkc_ref[...]
        nrm = jnp.sqrt(jnp.sum(kkr * kkr, axis=0, keepdims=True))
        kk = kkr / jnp.maximum(nrm, KK_EPS)
        al_s[t + off] = wcum * kk
        wcum = wcum * w_ref[0, t]
        inv = 1.0 / wcum
        be_s[t] = (kk * a_t) * inv
        kt_s[t] = (k_t * (1.0 + (a_t - 1.0) * kac_ref[...])) * inv
        rt_s[t] = wcum * r_ref[0, t]
        return wcum

    w_last = lax.fori_loop(0, tb, prep_step, jnp.ones((HEAD, LANES), F32))

    groups = [pl.ds(g * ROWS_SCAN, ROWS_SCAN) for g in range(HEAD // ROWS_SCAN)]

    def s_dot_al(rows, slot):
        acc = [None, None]
        for kx in range(HEAD):
            term = s_scr[kx, rows, :] * al_s[slot, pl.ds(kx, 1), :]
            acc[kx % 2] = term if acc[kx % 2] is None else acc[kx % 2] + term
        return acc[0] + acc[1]

    def step(s, sal_all):
        t = (tb - 1 - s) if reverse else s
        nxt = t if reverse else t + 1
        sal_next = []
        for rows, sal in zip(groups, sal_all):
            vv = v_ref[0, t, rows, :]
            yac = acc = None
            for kx in range(HEAD):
                kr = pl.ds(kx, 1)
                sn = s_scr[kx, rows, :] - sal * be_s[t, kr, :] + vv * kt_s[t, kr, :]
                s_scr[kx, rows, :] = sn
                ty = sn * rt_s[t, kr, :]
                ta = sn * al_s[nxt, kr, :]
                yac = ty if yac is None else yac + ty
                acc = ta if acc is None else acc + ta
            y_ref[0, t, rows, :] = yac
            sal_next.append(acc)
        return tuple(sal_next)

    first = (tb - 1 + off) if reverse else 0
    lax.fori_loop(0, tb, step, tuple(s_dot_al(rows, first) for rows in groups))

    wl_s = be_s.at[0]
    wl_s[...] = w_last
    for kx in range(HEAD):
        s_scr[kx] = s_scr[kx] * wl_s[pl.ds(kx, 1), :]

    @pl.when(i == pl.num_programs(1) - 1)
    def _():
        sfin_ref[0] = s_scr[...]


def _scan(r, k, v, w, a, kk_c, ka_c, s0, reverse):
    nt, t = r.shape[0], r.shape[1]
    tb = min(TB_SCAN, t)
    nb = t // tb
    tmap = (lambda c, i: (c, nb - 1 - i, 0, 0)) if reverse else (lambda c, i: (c, i, 0, 0))
    seq = pl.BlockSpec((1, tb, HEAD, LANES), tmap)
    cst = pl.BlockSpec((HEAD, LANES), lambda c, i: (0, 0))
    st = pl.BlockSpec((1, HEAD, HEAD, LANES), lambda c, i: (c, 0, 0, 0))
    return pl.pallas_call(
        functools.partial(_scan_kernel, tb=tb, reverse=reverse),
        grid=(nt, nb),
        in_specs=[seq] * 5 + [cst, cst, st],
        out_specs=[seq, st],
        out_shape=[jax.ShapeDtypeStruct(r.shape, F32), jax.ShapeDtypeStruct(s0.shape, F32)],
        scratch_shapes=[pltpu.VMEM((HEAD, HEAD, LANES), F32), pltpu.VMEM((tb + 1, HEAD, LANES), F32)]
        + [pltpu.VMEM((tb, HEAD, LANES), F32)] * 3,
        compiler_params=_params("arbitrary", "arbitrary"),
        name="scan_bwd" if reverse else "scan_fwd",
    )(r, k, v, w, a, kk_c, ka_c, s0)


def _gn_kernel(yf_ref, yb_ref, r_ref, k_ref, v_ref, af_ref, ab_ref, kac_ref, rkc_ref, g_ref, b_ref, o_ref):
    y = yf_ref[0] + yb_ref[0]
    mean = jnp.mean(y, axis=1, keepdims=True)
    dlt = y - mean
    var = jnp.mean(dlt * dlt, axis=1, keepdims=True)
    yn = dlt * lax.rsqrt(var + GN_EPS) * g_ref[...][None] + b_ref[...][None]
    r, k = r_ref[0], k_ref[0]
    kac = kac_ref[...][None]
    rk = r * rkc_ref[...][None]
    bo_f = jnp.sum(rk * (k * (1.0 + (af_ref[0] - 1.0) * kac)), axis=1, keepdims=True)
    bo_b = jnp.sum(rk * (k * (1.0 + (ab_ref[0] - 1.0) * kac)), axis=1, keepdims=True)
    o_ref[0] = yn + (bo_f + bo_b) * v_ref[0]


def _gn(yf, yb, r, k, v, af, ab, ka_c, rk_c, g_c, b_c):
    nt, t = r.shape[0], r.shape[1]
    tb = min(TB_GN, t)
    seq = pl.BlockSpec((1, tb, HEAD, LANES), lambda c, i: (c, i, 0, 0))
    cst = pl.BlockSpec((HEAD, LANES), lambda c, i: (0, 0))
    return pl.pallas_call(
        _gn_kernel,
        grid=(nt, t // tb),
        in_specs=[seq] * 7 + [cst] * 4,
        out_specs=seq,
        out_shape=jax.ShapeDtypeStruct(r.shape, F32),
        compiler_params=_params("arbitrary", "arbitrary"),
        name="gn",
    )(yf, yb, r, k, v, af, ab, ka_c, rk_c, g_c, b_c)


def _lru_kernel(x_ref, h0_ref, cw_ref, cb_ref, wr_ref, br_ref, wi_ref, bi_ref, lam_ref, o_ref, hfin_ref,
                xpad, af, uf, ab, ub, *, t_len, ch):
    cbk = x_ref.shape[-1]
    pad = SUBLANES
    xpad[0:pad, :] = jnp.zeros((pad, cbk), F32)
    xpad[t_len + pad:t_len + 2 * pad, :] = jnp.zeros((pad, cbk), F32)
    xpad[pad:t_len + pad, :] = x_ref[0]
    cw = cw_ref[...]
    for c0 in range(0, t_len, ch):
        xc = cb_ref[...] + xpad[c0 + pad - 2:c0 + pad - 2 + ch, :] * cw[0:1, :]
        for j in range(1, 4):
            xc = xc + xpad[c0 + pad - 2 + j:c0 + pad - 2 + j + ch, :] * cw[j:j + 1, :]
        xcb = xc.astype(BF16)
        for d, (a_s, u_s) in enumerate(((af, uf), (ab, ub))):
            rg = _sigmoid(jnp.dot(xcb, wr_ref[d, 0], preferred_element_type=F32) + br_ref[d:d + 1, :])
            ig = _sigmoid(jnp.dot(xcb, wi_ref[d, 0], preferred_element_type=F32) + bi_ref[d:d + 1, :])
            log_a = (-LRU_C * _softplus(-lam_ref[d:d + 1, :])) * rg
            a = jnp.exp(log_a)
            a_s[c0:c0 + ch, :] = a
            u_s[c0:c0 + ch, :] = jnp.sqrt(1.0 - a * a) * (ig * xc)

    rowi = lax.broadcasted_iota(jnp.int32, (SUBLANES, cbk), 0)

    def tile_scan(a, u, h_in, backward):
        for s in (1, 2, 4):
            sh = SUBLANES - s if backward else s
            a_s, u_s = pltpu.roll(a, sh, 0), pltpu.roll(u, sh, 0)
            m = (rowi < SUBLANES - s) if backward else (rowi >= s)
            u = jnp.where(m, u + a * u_s, u)
            a = jnp.where(m, a * a_s, a)
        h = a * h_in + u
        return h, (h[0:1, :] if backward else h[SUBLANES - 1:SUBLANES, :])

    def body(i, carry):
        hf, hb = carry
        of = pl.multiple_of(i * SUBLANES, SUBLANES)
        h, hf = tile_scan(af[pl.ds(of, SUBLANES), :], uf[pl.ds(of, SUBLANES), :], hf, False)
        af[pl.ds(of, SUBLANES), :] = h
        ob = pl.multiple_of(t_len - SUBLANES - i * SUBLANES, SUBLANES)
        h, hb = tile_scan(ab[pl.ds(ob, SUBLANES), :], ub[pl.ds(ob, SUBLANES), :], hb, True)
        ab[pl.ds(ob, SUBLANES), :] = h
        return hf, hb

    hf, hb = lax.fori_loop(0, t_len // SUBLANES, body, (h0_ref[0, 0:1, :], h0_ref[0, 1:2, :]), unroll=4)
    o_ref[0] = af[...] + ab[...]
    hfin_ref[0, 0:1, :] = hf
    hfin_ref[0, 1:2, :] = hb


def _lru(z3, h0, lw, w_a, w_b):
    b, t, _ = z3.shape
    cbk = CB_LRU
    ncb = w_b // cbk
    off = 4 * w_a // cbk
    ch = min(CH_LRU, t)
    wcol = lambda rows: pl.BlockSpec((rows, cbk), lambda bb, c: (0, c))
    wgate = pl.BlockSpec((2, 1, cbk, cbk), lambda bb, c: (0, c, 0, 0))
    st = pl.BlockSpec((1, 2, cbk), lambda bb, c: (bb, 0, c))
    return pl.pallas_call(
        functools.partial(_lru_kernel, t_len=t, ch=ch),
        grid=(b, ncb),
        in_specs=[pl.BlockSpec((1, t, cbk), lambda bb, c: (bb, 0, off + c)), st,
                  wcol(4), wcol(1), wgate, wcol(2), wgate, wcol(2), wcol(2)],
        out_specs=[pl.BlockSpec((1, t, cbk), lambda bb, c: (bb, 0, c)), st],
        out_shape=[jax.ShapeDtypeStruct((b, t, w_b), F32), jax.ShapeDtypeStruct((b, 2, w_b), F32)],
        scratch_shapes=[pltpu.VMEM((t + 2 * SUBLANES, cbk), F32)] + [pltpu.VMEM((t, cbk), F32)] * 4,
        compiler_params=_params("arbitrary", "arbitrary"),
        name="lru",
    )(z3, h0, lw["conv_w"], lw["conv_b"], lw["gr_bd"], lw["gr_b"], lw["gi_bd"], lw["gi_b"], lw["lam"])


def _out_kernel(y_ref, ga_ref, h_ref, gb_ref, ma_ref, mb_ref, x_ref, mod_ref, woa_ref, wob_ref, wo_ref,
                fg_ref, o_ref, *, final):
    ya = jnp.dot((y_ref[...] * _silu(ga_ref[...])).astype(BF16), woa_ref[...], preferred_element_type=F32)
    yb = jnp.dot((h_ref[...] * _silu(gb_ref[...])).astype(BF16), wob_ref[...], preferred_element_type=F32)
    merged = _sigmoid(ma_ref[...]) * ya + _sigmoid(mb_ref[...]) * yb
    out = jnp.dot(merged.astype(BF16), wo_ref[...], preferred_element_type=F32)
    xn = x_ref[...] + mod_ref[0, 2:3, :] * out
    if final:
        xn = xn * lax.rsqrt(jnp.mean(xn * xn, axis=-1, keepdims=True) + RMS_EPS) * fg_ref[...]
    o_ref[...] = xn


def _out(y2, z2, h2, x2, mod, lw, final_g, tm, mod_row, final, w_a, w_b):
    m, d = x2.shape
    once = dict(pipeline_mode=pl.Buffered(1))
    zoff_ga = 3 * w_a // w_a
    zoff_gb = (4 * w_a + w_b) // w_b
    zoff_ma = (4 * w_a + 2 * w_b) // d
    return pl.pallas_call(
        functools.partial(_out_kernel, final=final),
        grid=(m // tm,),
        in_specs=[
            pl.BlockSpec((tm, w_a), lambda i: (i, 0)),
            pl.BlockSpec((tm, w_a), lambda i: (i, zoff_ga)),
            pl.BlockSpec((tm, w_b), lambda i: (i, 0)),
            pl.BlockSpec((tm, w_b), lambda i: (i, zoff_gb)),
            pl.BlockSpec((tm, d), lambda i: (i, zoff_ma)),
            pl.BlockSpec((tm, d), lambda i: (i, zoff_ma + 1)),
            pl.BlockSpec((tm, d), lambda i: (i, 0)),
            pl.BlockSpec((1, 3, d), lambda i: (mod_row(i), 0, 0)),
            pl.BlockSpec((w_a, d), lambda i: (0, 0), **once),
            pl.BlockSpec((w_b, d), lambda i: (0, 0), **once),
            pl.BlockSpec((d, d), lambda i: (0, 0), **once),
            pl.BlockSpec((1, d), lambda i: (0, 0)),
        ],
        out_specs=pl.BlockSpec((tm, d), lambda i: (i, 0)),
        out_shape=jax.ShapeDtypeStruct((m, d), F32),
        compiler_params=_params("arbitrary"),
        name="out_final" if final else "out",
    )(y2, z2, h2, z2, z2, z2, x2, mod, lw["w_out_a"], lw["w_out_b"], lw["w_out"], final_g.reshape(1, d))


def _to_chains(x, heads):
    b, t, _ = x.shape
    nt = b // CHAIN_BATCH
    x = x.reshape(nt, CHAIN_BATCH, t, heads, HEAD)
    return x.transpose(0, 2, 4, 1, 3).reshape(nt, t, HEAD, CHAIN_BATCH * heads)


def _from_chains(x, heads):
    nt, t = x.shape[0], x.shape[1]
    x = x.reshape(nt, t, HEAD, CHAIN_BATCH, heads)
    return x.transpose(0, 3, 1, 4, 2).reshape(nt * CHAIN_BATCH, t, heads * HEAD)


def _chain_const(p, heads):
    t = p.reshape(heads, HEAD).T
    return jnp.broadcast_to(t[:, None, :], (HEAD, CHAIN_BATCH, heads)).reshape(HEAD, CHAIN_BATCH * heads)


def _state_to_chains(s):
    b, h = s.shape[0], s.shape[1]
    nt = b // CHAIN_BATCH
    s = s.reshape(nt, CHAIN_BATCH, h, HEAD, HEAD)
    return s.transpose(0, 4, 3, 1, 2).reshape(nt, HEAD, HEAD, CHAIN_BATCH * h)


def _state_from_chains(s, heads):
    nt = s.shape[0]
    s = s.reshape(nt, HEAD, HEAD, CHAIN_BATCH, heads)
    return s.transpose(0, 3, 4, 2, 1).reshape(nt * CHAIN_BATCH, heads, HEAD, HEAD)


def _block_diag(w, per):
    h, n, _ = w.shape
    g = h // per
    eye = jnp.eye(per, dtype=w.dtype)
    wg = w.reshape(g, per, n, n)
    return (wg[:, :, :, None, :] * eye[None, :, None, :, None]).reshape(g, per * n, per * n)


def _layer_weights(l, n_layers, heads, w_in, mu_rkv, dec_w0, dec_w1, dec_w2, iclr_w0, iclr_w1, iclr_w2,
                   vres_w0, vres_w1, vres_w2, k_k, k_a, r_k, lnx_g, lnx_b, w_out_a, conv_w, conv_b, gr_w,
                   gr_b, gi_w, gi_b, lru_lambda, w_out_b, w_out):
    d = w_in.shape[1]
    w_a = heads * HEAD
    r_dec, r_iclr, r_vres = dec_w1.shape[-1], iclr_w1.shape[-1], vres_w1.shape[-1]
    assert (r_dec, r_iclr, r_vres) == (96, 96, 64), "LoRA column layout assumes ranks 96/96/64"
    has_vres = l > 0
    vw1 = vres_w1[l - 1] if has_vres else jnp.zeros((d, r_vres), F32)
    used = 2 * r_dec + 2 * r_iclr + r_vres
    w_ext = jnp.concatenate(
        [w_in[l], dec_w1[l, 0], dec_w1[l, 1], iclr_w1[l, 0], iclr_w1[l, 1], vw1,
         jnp.zeros((d, LORA_COLS - used), F32)], axis=1).astype(BF16)
    dec_w2p = jnp.zeros((2, 256, w_a), F32)
    iclr_w2p = jnp.zeros((2, 256, w_a), F32)
    for dd in range(2):
        dec_w2p = dec_w2p.at[dd, r_dec * dd:r_dec * (dd + 1)].set(dec_w2[l, dd])
        o = 2 * r_dec - 128 + r_iclr * dd
        iclr_w2p = iclr_w2p.at[dd, o:o + r_iclr].set(iclr_w2[l, dd])
    lw = dict(
        w_ext=w_ext, mu=mu_rkv[l], dec_w0=dec_w0[l], dec_w2p=dec_w2p.astype(BF16), iclr_w0=iclr_w0[l],
        iclr_w2p=iclr_w2p.astype(BF16),
        kk_c=_chain_const(k_k[l], heads), ka_c=_chain_const(k_a[l], heads),
        rk_c=_chain_const(r_k[l].reshape(-1), heads), g_c=_chain_const(lnx_g[l], heads),
        b_c=_chain_const(lnx_b[l], heads),
        conv_w=conv_w[l], conv_b=conv_b[l].reshape(1, -1),
        gr_bd=jnp.stack([_block_diag(gr_w[l, dd], CB_LRU // gr_w.shape[-1]) for dd in range(2)]).astype(BF16),
        gi_bd=jnp.stack([_block_diag(gi_w[l, dd], CB_LRU // gi_w.shape[-1]) for dd in range(2)]).astype(BF16),
        gr_b=gr_b[l], gi_b=gi_b[l], lam=lru_lambda[l],
        w_out_a=w_out_a[l].astype(BF16), w_out_b=w_out_b[l].astype(BF16), w_out=w_out[l].astype(BF16),
    )
    if has_vres:
        vres_w2p = jnp.zeros((128, w_a), F32).at[0:r_vres].set(vres_w2[l - 1])
        lw.update(vres_w0=vres_w0[l - 1].reshape(1, -1), vres_w2p=vres_w2p.astype(BF16))
    return lw


def _trunk(x, mods, mod_row_in, mod_row_out, tm_in, tm_out, grid_w, rwkv_init, lru_init, layers, final_g,
           heads, w_b):
    b, t, d = x.shape
    w_a = heads * HEAD
    n_layers = len(layers)
    x2 = x.reshape(b * t, d)
    v_first = None
    s_out, h_out = [], []
    for l, lw in enumerate(layers):
        n_ext = lw["w_ext"].shape[1]
        z2 = _inproj(x2, mods[l], lw["norm_g"], lw["w_ext"], tm_in, mod_row_in)
        z3 = z2.reshape(b, t, n_ext)
        r, k, v, w_f, w_bk, a_f, a_bk = _prep(z3, v_first, lw, grid_w, w_a, n_ext - LORA_COLS)
        if v_first is None:
            v_first = v
        rc, kc, vc = (_to_chains(q, heads) for q in (r, k, v))
        afc, abc = _to_chains(a_f, heads), _to_chains(a_bk, heads)
        y_f, s_f = _scan(rc, kc, vc, _to_chains(w_f, heads), afc, lw["kk_c"], lw["ka_c"], rwkv_init[l][0], False)
        y_b, s_b = _scan(rc, kc, vc, _to_chains(w_bk, heads), abc, lw["kk_c"], lw["ka_c"], rwkv_init[l][1], True)
        y_rwkv = _from_chains(
            _gn(y_f, y_b, rc, kc, vc, afc, abc, lw["ka_c"], lw["rk_c"], lw["g_c"], lw["b_c"]), heads)
        h_sum, h_fin = _lru(z3, lru_init[l], lw, w_a, w_b)
        x2 = _out(y_rwkv.reshape(b * t, w_a), z2, h_sum.reshape(b * t, w_b), x2, mods[l], lw, final_g,
                  tm_out, mod_row_out, l == n_layers - 1, w_a, w_b)
        s_out.append((s_f, s_b))
        h_out.append(h_fin)
    return x2.reshape(b, t, d), s_out, h_out


def kernel(x_prompt, x_sample, c, state_rwkv, state_lru, c_ctx, norm_g, ada_w, ada_b, w_in, mu_rkv, dec_w0,
           dec_w1, dec_w2, iclr_w0, iclr_w1, iclr_w2, vres_w0, vres_w1, vres_w2, k_k, k_a, r_k, lnx_g, lnx_b,
           w_out_a, conv_w, conv_b, gr_w, gr_b, gi_w, gi_b, lru_lambda, w_out_b, w_out, final_g):
    n_layers, d = norm_g.shape
    bp, tp, _ = x_prompt.shape
    bs, ts, _ = x_sample.shape
    heads = r_k.shape[1]
    w_b = conv_w.shape[-1]
    grid_w = int(round(ts ** 0.5))
    assert grid_w * grid_w == ts and heads * CHAIN_BATCH == LANES
    assert bp % CHAIN_BATCH == 0 and bs % CHAIN_BATCH == 0

    layers = []
    for l in range(n_layers):
        lw = _layer_weights(l, n_layers, heads, w_in, mu_rkv, dec_w0, dec_w1, dec_w2, iclr_w0, iclr_w1,
                            iclr_w2, vres_w0, vres_w1, vres_w2, k_k, k_a, r_k, lnx_g, lnx_b, w_out_a, conv_w,
                            conv_b, gr_w, gr_b, gi_w, gi_b, lru_lambda, w_out_b, w_out)
        lw["norm_g"] = norm_g[l]
        layers.append(lw)

    ctx_row = bs
    n_rows = -(-(bs + 1) // SUBLANES) * SUBLANES
    cond = jnp.zeros((n_rows, d), F32).at[0:bs].set(c).at[ctx_row].set(c_ctx)
    mod = _ada_mod(cond, ada_w, ada_b).reshape(n_layers, n_rows, 3, d)
    mods = [mod[l] for l in range(n_layers)]

    zero_s = jnp.zeros((bp // CHAIN_BATCH, HEAD, HEAD, LANES), F32)
    tm_p = min(TM_IN, bp * tp)
    tm_po = min(TM_OUT, bp * tp)
    x_ctx, s_ctx, h_ctx = _trunk(
        x_prompt, mods, lambda i: ctx_row, lambda i: ctx_row, tm_p, tm_po, 0,
        [(zero_s, zero_s)] * n_layers, [jnp.zeros((bp, 2, w_b), F32)] * n_layers, layers, final_g, heads, w_b)

    tm_s = min(TM_IN, ts)
    tm_so = min(TM_OUT, ts)
    assert ts % tm_s == 0 and ts % tm_so == 0
    rwkv_init = [tuple(_state_to_chains(state_rwkv[:, l, dd]) for dd in range(2)) for l in range(n_layers)]
    x_lat, _, _ = _trunk(
        x_sample, mods, lambda i: i // (ts // tm_s), lambda i: i // (ts // tm_so), tm_s, tm_so, grid_w,
        rwkv_init, [state_lru[:, l] for l in range(n_layers)], layers, final_g, heads, w_b)

    new_rwkv = jnp.stack(
        [jnp.stack([_state_from_chains(s_ctx[l][dd], heads) for dd in range(2)], axis=1)
         for l in range(n_layers)], axis=1)
    new_lru = jnp.stack(h_ctx, axis=1)
    return x_ctx, x_lat, new_rwkv, new_lru
```

```python
import functools
import math

import jax
import jax.numpy as jnp
from jax import lax
from jax.experimental import pallas as pl
from jax.experimental.pallas import tpu as pltpu

F32 = jnp.float32
BF16 = jnp.bfloat16

LANES = 128
SUBLANES = 8
HEAD = 64
CHAIN_BATCH = 8
LORA_COLS = 512
RMS_EPS = 1e-6
GN_EPS = 64e-5
LRU_C = 8.0
KK_EPS = 1e-12
DECAY_SCALE = math.exp(-0.5)
VMEM_LIMIT = 56 * 1024 * 1024

TM_IN = 1024
TN_IN = 1536
ROWS_NORM = 16
TM_OUT = 256
TT_PREP = 512
ROWS_PREP = 64
CW_PREP = 256
TB_SCAN = 32
ROWS_SCAN = 32
TB_GN = 32
CB_LRU = 256
CH_LRU = 512


def _params(*sem):
    return pltpu.CompilerParams(dimension_semantics=sem, vmem_limit_bytes=VMEM_LIMIT)


def _softplus(x):
    return jnp.maximum(x, 0.0) + jnp.log1p(jnp.exp(-jnp.abs(x)))


def _sigmoid(x):
    return 0.5 * jnp.tanh(0.5 * x) + 0.5


def _silu(x):
    return x * _sigmoid(x)


def _ada_kernel(c_ref, w_ref, b_ref, o_ref):
    h = _silu(c_ref[...]).astype(BF16)
    o_ref[0] = jnp.dot(h, w_ref[0].astype(BF16), preferred_element_type=F32) + b_ref[0]


def _ada_mod(cond, ada_w, ada_b):
    n_l, d, n = ada_w.shape
    bc = cond.shape[0]
    tn = math.gcd(n, 512)
    return pl.pallas_call(
        _ada_kernel,
        grid=(n_l, n // tn),
        in_specs=[
            pl.BlockSpec((bc, d), lambda l, j: (0, 0)),
            pl.BlockSpec((1, d, tn), lambda l, j: (l, 0, j)),
            pl.BlockSpec((1, 1, tn), lambda l, j: (l, 0, j)),
        ],
        out_specs=pl.BlockSpec((1, bc, tn), lambda l, j: (l, 0, j)),
        out_shape=jax.ShapeDtypeStruct((n_l, bc, n), F32),
        compiler_params=_params("arbitrary", "arbitrary"),
        name="ada",
    )(cond, ada_w, ada_b.reshape(n_l, 1, n))


def _inproj_kernel(x_ref, mod_ref, g_ref, w_ref, o_ref, xm_ref):
    @pl.when(pl.program_id(1) == 0)
    def _():
        gain = g_ref[...]
        scale = 1.0 + mod_ref[0, 1:2, :]
        shift = mod_ref[0, 0:1, :]

        def rows_body(c, carry):
            rows = pl.ds(pl.multiple_of(c * ROWS_NORM, ROWS_NORM), ROWS_NORM)
            x = x_ref[rows, :]
            y = x * lax.rsqrt(jnp.mean(x * x, axis=-1, keepdims=True) + RMS_EPS) * gain
            xm_ref[rows, :] = (y * scale + shift).astype(BF16)
            return carry

        lax.fori_loop(0, x_ref.shape[0] // ROWS_NORM, rows_body, 0, unroll=8)

    o_ref[...] = jnp.dot(xm_ref[...], w_ref[0], preferred_element_type=F32)


def _col_tile(n, cap):
    return 256 * max(q for q in range(1, cap // 256 + 1) if (n // 256) % q == 0)


def _inproj(x2, mod, norm_g, w_tiles, tm, mod_row):
    m, d = x2.shape
    nj, _, tn = w_tiles.shape
    return pl.pallas_call(
        _inproj_kernel,
        grid=(m // tm, nj),
        in_specs=[
            pl.BlockSpec((tm, d), lambda i, j: (i, 0)),
            pl.BlockSpec((1, 3, d), lambda i, j: (mod_row(i), 0, 0)),
            pl.BlockSpec((1, d), lambda i, j: (0, 0)),
            pl.BlockSpec((1, d, tn), lambda i, j: (j, 0, 0)),
        ],
        out_specs=pl.BlockSpec((tm, tn), lambda i, j: (i, j)),
        out_shape=jax.ShapeDtypeStruct((m, nj * tn), F32),
        scratch_shapes=[pltpu.VMEM((tm, d), BF16)],
        compiler_params=_params("arbitrary", "arbitrary"),
        name="inproj",
    )(x2, mod, norm_g.reshape(1, d), w_tiles)


def _prep_kernel(*refs, tt, cw, grid_w, n_t, has_vres):
    it = iter(refs)
    zr, zk, zv = next(it), next(it), next(it)
    halos = [next(it) for _ in range(6)] if grid_w else [None] * 6
    lo_ref = next(it)
    vfirst = next(it) if has_vres else None
    mu, dw0, dw2, iw0, iw2 = (next(it) for _ in range(5))
    vw0, vw2 = (next(it), next(it)) if has_vres else (None, None)
    r_o, k_o, v_o, wf_o, wb_o, af_o, ab_o = (next(it) for _ in range(7))
    td_s, li_s, lv_s, er_s, ek_s, ev_s = (next(it) for _ in range(6))

    @pl.when(pl.program_id(2) == 0)
    def _():
        td_s[...] = jnp.tanh(lo_ref[0, :, 0:256]).astype(BF16)
        li_s[...] = lo_ref[0, :, 128:384].astype(BF16)
        lv_s[...] = lo_ref[0, :, 384:512].astype(BF16)

    pad = grid_w if grid_w else SUBLANES
    for e_s, z_ref, hp, hn in ((er_s, zr, halos[0], halos[1]), (ek_s, zk, halos[2], halos[3]),
                               (ev_s, zv, halos[4], halos[5])):
        if grid_w:
            i = pl.program_id(1)
            e_s[0:pad, :] = jnp.where(i > 0, hp[0], 0.0)
            e_s[pad + tt:2 * pad + tt, :] = jnp.where(i < n_t - 1, hn[0], 0.0)
        else:
            e_s[0:pad, :] = jnp.zeros((pad, cw), F32)
            e_s[pad + tt:2 * pad + tt, :] = jnp.zeros((pad, cw), F32)
        e_s[pad:pad + tt, :] = z_ref[0]

    rc = min(tt, ROWS_PREP)
    col = lax.rem(lax.broadcasted_iota(jnp.int32, (rc, cw), 0), grid_w) if grid_w else None

    def mix(e_s, c0, j):
        z = e_s[pad + c0:pad + c0 + rc, :]
        before = e_s[pad + c0 - 1:pad + c0 - 1 + rc, :]
        after = e_s[pad + c0 + 1:pad + c0 + 1 + rc, :]
        if grid_w:
            up = e_s[c0:c0 + rc, :]
            down = e_s[2 * pad + c0:2 * pad + c0 + rc, :]
            left = jnp.where(col == 0, 0.0, before)
            right = jnp.where(col == grid_w - 1, 0.0, after)
            sh = 0.25 * (up + down + left + right)
        else:
            sh = 0.5 * (before + after)
        return z + (sh - z) * mu[j:j + 1, :]

    for c0 in range(0, tt, rc):
        rows = pl.ds(c0, rc)
        r_o[0, rows, :] = mix(er_s, c0, 0)
        k_o[0, rows, :] = mix(ek_s, c0, 1)
        v = mix(ev_s, c0, 2)
        if has_vres:
            gate = _sigmoid(vw0[...] + jnp.dot(lv_s[rows, :], vw2[...], preferred_element_type=F32))
            v = v + (vfirst[0, rows, :] - v) * gate
        v_o[0, rows, :] = v
        td = td_s[rows, :]
        li = li_s[rows, :]
        for d, (w_o, a_o) in enumerate(((wf_o, af_o), (wb_o, ab_o))):
            xd = dw0[d:d + 1, :] + jnp.dot(td, dw2[d], preferred_element_type=F32)
            w_o[0, rows, :] = jnp.exp(-DECAY_SCALE * _sigmoid(xd))
            a_o[0, rows, :] = _sigmoid(iw0[d:d + 1, :] + jnp.dot(li, iw2[d], preferred_element_type=F32))


def _prep(z3, v_first, lw, grid_w, w_a, n_main):
    b, t, _ = z3.shape
    cw = CW_PREP
    nc = w_a // cw
    has_vres = v_first is not None
    if grid_w:
        tt = min(TT_PREP, t)
        assert min(tt, ROWS_PREP) % grid_w == 0 and grid_w % SUBLANES == 0 and t % tt == 0
    else:
        tt = t
    n_t = t // tt
    blk = lambda off: pl.BlockSpec((1, tt, cw), lambda bb, i, c, off=off: (bb, i, off + c))
    in_specs = [blk(0), blk(nc), blk(2 * nc)]
    args = [z3, z3, z3]
    if grid_w:
        per, last = tt // grid_w, t // grid_w - 1
        for off in (0, nc, 2 * nc):
            in_specs.append(pl.BlockSpec(
                (1, grid_w, cw),
                lambda bb, i, c, off=off: (bb, jnp.maximum(i * per - 1, 0), off + c)))
            in_specs.append(pl.BlockSpec(
                (1, grid_w, cw),
                lambda bb, i, c, off=off: (bb, jnp.minimum((i + 1) * per, last), off + c)))
            args += [z3, z3]
    in_specs.append(pl.BlockSpec((1, tt, LORA_COLS), lambda bb, i, c: (bb, i, n_main // LORA_COLS)))
    args.append(z3)
    if has_vres:
        in_specs.append(blk(0))
        args.append(v_first)
    wcol = lambda shape: pl.BlockSpec(shape, lambda bb, i, c: (0,) * (len(shape) - 1) + (c,))
    in_specs += [wcol((3, cw)), wcol((2, cw)), wcol((2, 256, cw)), wcol((2, cw)), wcol((2, 256, cw))]
    args += [lw["mu"], lw["dec_w0"], lw["dec_w2p"], lw["iclr_w0"], lw["iclr_w2p"]]
    if has_vres:
        in_specs += [wcol((1, cw)), wcol((128, cw))]
        args += [lw["vres_w0"], lw["vres_w2p"]]
    out = jax.ShapeDtypeStruct((b, t, w_a), F32)
    return pl.pallas_call(
        functools.partial(_prep_kernel, tt=tt, cw=cw, grid_w=grid_w, n_t=n_t, has_vres=has_vres),
        grid=(b, n_t, nc),
        in_specs=in_specs,
        out_specs=[blk(0)] * 7,
        out_shape=[out] * 7,
        scratch_shapes=[pltpu.VMEM((tt, 256), BF16), pltpu.VMEM((tt, 256), BF16), pltpu.VMEM((tt, 128), BF16)]
        + [pltpu.VMEM((tt + 2 * (grid_w if grid_w else SUBLANES), cw), F32)] * 3,
        compiler_params=_params("arbitrary", "arbitrary", "arbitrary"),
        name="prep",
    )(*args)


def _scan_kernel(r_ref, k_ref, v_ref, w_ref, a_ref, kkc_ref, kac_ref, s0_ref, y_ref, sfin_ref,
                 s_scr, al_s, be_s, kt_s, rt_s, *, tb, reverse):
    i = pl.program_id(1)

    @pl.when(i == 0)
    def _():
        s_scr[...] = s0_ref[0]

    off = 1 if reverse else 0
    al_s[0 if reverse else tb] = jnp.zeros((HEAD, LANES), F32)

    def prep_step(s, wcum):
        t = (tb - 1 - s) if reverse else s
        k_t, a_t = k_ref[0, t], a_ref[0, t]
        kkr = k_t * kkc_ref[...]
        nrm = jnp.sqrt(jnp.sum(kkr * kkr, axis=0, keepdims=True))
        kk = kkr / jnp.maximum(nrm, KK_EPS)
        al_s[t + off] = wcum * kk
        wcum = wcum * w_ref[0, t]
        inv = 1.0 / wcum
        be_s[t] = (kk * a_t) * inv
        kt_s[t] = (k_t * (1.0 + (a_t - 1.0) * kac_ref[...])) * inv
        rt_s[t] = wcum * r_ref[0, t]
        return wcum

    w_last = lax.fori_loop(0, tb, prep_step, jnp.ones((HEAD, LANES), F32))

    groups = [pl.ds(g * ROWS_SCAN, ROWS_SCAN) for g in range(HEAD // ROWS_SCAN)]

    def s_dot_al(rows, slot):
        acc = [None, None]
        for kx in range(HEAD):
            term = s_scr[kx, rows, :] * al_s[slot, pl.ds(kx, 1), :]
            acc[kx % 2] = term if acc[kx % 2] is None else acc[kx % 2] + term
        return acc[0] + acc[1]

    def step(s, sal_all):
        t = (tb - 1 - s) if reverse else s
        nxt = t if reverse else t + 1
        sal_next = []
        for rows, sal in zip(groups, sal_all):
            vv = v_ref[0, t, rows, :]
            yac = acc = None
            for kx in range(HEAD):
                kr = pl.ds(kx, 1)
                sn = s_scr[kx, rows, :] - sal * be_s[t, kr, :] + vv * kt_s[t, kr, :]
                s_scr[kx, rows, :] = sn
                ty = sn * rt_s[t, kr, :]
                ta = sn * al_s[nxt, kr, :]
                yac = ty if yac is None else yac + ty
                acc = ta if acc is None else acc + ta
            y_ref[0, t, rows, :] = yac
            sal_next.append(acc)
        return tuple(sal_next)

    first = (tb - 1 + off) if reverse else 0
    lax.fori_loop(0, tb, step, tuple(s_dot_al(rows, first) for rows in groups))

    wl_s = be_s.at[0]
    wl_s[...] = w_last
    for kx in range(HEAD):
        s_scr[kx] = s_scr[kx] * wl_s[pl.ds(kx, 1), :]

    @pl.when(i == pl.num_programs(1) - 1)
    def _():
        sfin_ref[0] = s_scr[...]


def _scan(r, k, v, w, a, kk_c, ka_c, s0, reverse):
    nt, t = r.shape[0], r.shape[1]
    tb = min(TB_SCAN, t)
    nb = t // tb
    tmap = (lambda c, i: (c, nb - 1 - i, 0, 0)) if reverse else (lambda c, i: (c, i, 0, 0))
    seq = pl.BlockSpec((1, tb, HEAD, LANES), tmap)
    cst = pl.BlockSpec((HEAD, LANES), lambda c, i: (0, 0))
    st = pl.BlockSpec((1, HEAD, HEAD, LANES), lambda c, i: (c, 0, 0, 0))
    return pl.pallas_call(
        functools.partial(_scan_kernel, tb=tb, reverse=reverse),
        grid=(nt, nb),
        in_specs=[seq] * 5 + [cst, cst, st],
        out_specs=[seq, st],
        out_shape=[jax.ShapeDtypeStruct(r.shape, F32), jax.ShapeDtypeStruct(s0.shape, F32)],
        scratch_shapes=[pltpu.VMEM((HEAD, HEAD, LANES), F32), pltpu.VMEM((tb + 1, HEAD, LANES), F32)]
        + [pltpu.VMEM((tb, HEAD, LANES), F32)] * 3,
        compiler_params=_params("arbitrary", "arbitrary"),
        name="scan_bwd" if reverse else "scan_fwd",
    )(r, k, v, w, a, kk_c, ka_c, s0)


def _gn_kernel(yf_ref, yb_ref, r_ref, k_ref, v_ref, af_ref, ab_ref, kac_ref, rkc_ref, g_ref, b_ref, o_ref):
    y = yf_ref[0] + yb_ref[0]
    mean = jnp.mean(y, axis=1, keepdims=True)
    dlt = y - mean
    var = jnp.mean(dlt * dlt, axis=1, keepdims=True)
    yn = dlt * lax.rsqrt(var + GN_EPS) * g_ref[...][None] + b_ref[...][None]
    r, k = r_ref[0], k_ref[0]
    kac = kac_ref[...][None]
    rk = r * rkc_ref[...][None]
    bo_f = jnp.sum(rk * (k * (1.0 + (af_ref[0] - 1.0) * kac)), axis=1, keepdims=True)
    bo_b = jnp.sum(rk * (k * (1.0 + (ab_ref[0] - 1.0) * kac)), axis=1, keepdims=True)
    o_ref[0] = yn + (bo_f + bo_b) * v_ref[0]


def _gn(yf, yb, r, k, v, af, ab, ka_c, rk_c, g_c, b_c):
    nt, t = r.shape[0], r.shape[1]
    tb = min(TB_GN, t)
    seq = pl.BlockSpec((1, tb, HEAD, LANES), lambda c, i: (c, i, 0, 0))
    cst = pl.BlockSpec((HEAD, LANES), lambda c, i: (0, 0))
    return pl.pallas_call(
        _gn_kernel,
        grid=(nt, t // tb),
        in_specs=[seq] * 7 + [cst] * 4,
        out_specs=seq,
        out_shape=jax.ShapeDtypeStruct(r.shape, F32),
        compiler_params=_params("arbitrary", "arbitrary"),
        name="gn",
    )(yf, yb, r, k, v, af, ab, ka_c, rk_c, g_c, b_c)


def _lru_kernel(x_ref, h0_ref, cw_ref, cb_ref, wr_ref, br_ref, wi_ref, bi_ref, lam_ref, o_ref, hfin_ref,
                xpad, af, uf, ab, ub, *, t_len, ch):
    cbk = x_ref.shape[-1]
    pad = SUBLANES
    xpad[0:pad, :] = jnp.zeros((pad, cbk), F32)
    xpad[t_len + pad:t_len + 2 * pad, :] = jnp.zeros((pad, cbk), F32)
    xpad[pad:t_len + pad, :] = x_ref[0]
    cw = cw_ref[...]
    for c0 in range(0, t_len, ch):
        xc = cb_ref[...] + xpad[c0 + pad - 2:c0 + pad - 2 + ch, :] * cw[0:1, :]
        for j in range(1, 4):
            xc = xc + xpad[c0 + pad - 2 + j:c0 + pad - 2 + j + ch, :] * cw[j:j + 1, :]
        xcb = xc.astype(BF16)
        for d, (a_s, u_s) in enumerate(((af, uf), (ab, ub))):
            rg = _sigmoid(jnp.dot(xcb, wr_ref[d, 0], preferred_element_type=F32) + br_ref[d:d + 1, :])
            ig = _sigmoid(jnp.dot(xcb, wi_ref[d, 0], preferred_element_type=F32) + bi_ref[d:d + 1, :])
            log_a = (-LRU_C * _softplus(-lam_ref[d:d + 1, :])) * rg
            a = jnp.exp(log_a)
            a_s[c0:c0 + ch, :] = a
            u_s[c0:c0 + ch, :] = jnp.sqrt(1.0 - a * a) * (ig * xc)

    rowi = lax.broadcasted_iota(jnp.int32, (SUBLANES, cbk), 0)

    def tile_scan(a, u, h_in, backward):
        for s in (1, 2, 4):
            sh = SUBLANES - s if backward else s
            a_s, u_s = pltpu.roll(a, sh, 0), pltpu.roll(u, sh, 0)
            m = (rowi < SUBLANES - s) if backward else (rowi >= s)
            u = jnp.where(m, u + a * u_s, u)
            a = jnp.where(m, a * a_s, a)
        h = a * h_in + u
        return h, (h[0:1, :] if backward else h[SUBLANES - 1:SUBLANES, :])

    def body(i, carry):
        hf, hb = carry
        of = pl.multiple_of(i * SUBLANES, SUBLANES)
        h, hf = tile_scan(af[pl.ds(of, SUBLANES), :], uf[pl.ds(of, SUBLANES), :], hf, False)
        af[pl.ds(of, SUBLANES), :] = h
        ob = pl.multiple_of(t_len - SUBLANES - i * SUBLANES, SUBLANES)
        h, hb = tile_scan(ab[pl.ds(ob, SUBLANES), :], ub[pl.ds(ob, SUBLANES), :], hb, True)
        ab[pl.ds(ob, SUBLANES), :] = h
        return hf, hb

    hf, hb = lax.fori_loop(0, t_len // SUBLANES, body, (h0_ref[0, 0:1, :], h0_ref[0, 1:2, :]), unroll=4)
    o_ref[0] = af[...] + ab[...]
    hfin_ref[0, 0:1, :] = hf
    hfin_ref[0, 1:2, :] = hb


def _lru(z3, h0, lw, w_a, w_b):
    b, t, _ = z3.shape
    cbk = CB_LRU
    ncb = w_b // cbk
    off = 4 * w_a // cbk
    ch = min(CH_LRU, t)
    wcol = lambda rows: pl.BlockSpec((rows, cbk), lambda bb, c: (0, c))
    wgate = pl.BlockSpec((2, 1, cbk, cbk), lambda bb, c: (0, c, 0, 0))
    st = pl.BlockSpec((1, 2, cbk), lambda bb, c: (bb, 0, c))
    return pl.pallas_call(
        functools.partial(_lru_kernel, t_len=t, ch=ch),
        grid=(b, ncb),
        in_specs=[pl.BlockSpec((1, t, cbk), lambda bb, c: (bb, 0, off + c)), st,
                  wcol(4), wcol(1), wgate, wcol(2), wgate, wcol(2), wcol(2)],
        out_specs=[pl.BlockSpec((1, t, cbk), lambda bb, c: (bb, 0, c)), st],
        out_shape=[jax.ShapeDtypeStruct((b, t, w_b), F32), jax.ShapeDtypeStruct((b, 2, w_b), F32)],
        scratch_shapes=[pltpu.VMEM((t + 2 * SUBLANES, cbk), F32)] + [pltpu.VMEM((t, cbk), F32)] * 4,
        compiler_params=_params("arbitrary", "arbitrary"),
        name="lru",
    )(z3, h0, lw["conv_w"], lw["conv_b"], lw["gr_bd"], lw["gr_b"], lw["gi_bd"], lw["gi_b"], lw["lam"])


def _out_kernel(y_ref, ga_ref, h_ref, gb_ref, ma_ref, mb_ref, x_ref, mod_ref, woa_ref, wob_ref, wo_ref,
                fg_ref, o_ref, *, final):
    ya = jnp.dot((y_ref[...] * _silu(ga_ref[...])).astype(BF16), woa_ref[...], preferred_element_type=F32)
    yb = jnp.dot((h_ref[...] * _silu(gb_ref[...])).astype(BF16), wob_ref[...], preferred_element_type=F32)
    merged = _sigmoid(ma_ref[...]) * ya + _sigmoid(mb_ref[...]) * yb
    out = jnp.dot(merged.astype(BF16), wo_ref[...], preferred_element_type=F32)
    xn = x_ref[...] + mod_ref[0, 2:3, :] * out
    if final:
        xn = xn * lax.rsqrt(jnp.mean(xn * xn, axis=-1, keepdims=True) + RMS_EPS) * fg_ref[...]
    o_ref[...] = xn


def _out(y2, z2, h2, x2, mod, lw, final_g, tm, mod_row, final, w_a, w_b):
    m, d = x2.shape
    once = dict(pipeline_mode=pl.Buffered(1))
    zoff_ga = 3 * w_a // w_a
    zoff_gb = (4 * w_a + w_b) // w_b
    zoff_ma = (4 * w_a + 2 * w_b) // d
    return pl.pallas_call(
        functools.partial(_out_kernel, final=final),
        grid=(m // tm,),
        in_specs=[
            pl.BlockSpec((tm, w_a), lambda i: (i, 0)),
            pl.BlockSpec((tm, w_a), lambda i: (i, zoff_ga)),
            pl.BlockSpec((tm, w_b), lambda i: (i, 0)),
            pl.BlockSpec((tm, w_b), lambda i: (i, zoff_gb)),
            pl.BlockSpec((tm, d), lambda i: (i, zoff_ma)),
            pl.BlockSpec((tm, d), lambda i: (i, zoff_ma + 1)),
            pl.BlockSpec((tm, d), lambda i: (i, 0)),
            pl.BlockSpec((1, 3, d), lambda i: (mod_row(i), 0, 0)),
            pl.BlockSpec((w_a, d), lambda i: (0, 0), **once),
            pl.BlockSpec((w_b, d), lambda i: (0, 0), **once),
            pl.BlockSpec((d, d), lambda i: (0, 0), **once),
            pl.BlockSpec((1, d), lambda i: (0, 0)),
        ],
        out_specs=pl.BlockSpec((tm, d), lambda i: (i, 0)),
        out_shape=jax.ShapeDtypeStruct((m, d), F32),
        compiler_params=_params("arbitrary"),
        name="out_final" if final else "out",
    )(y2, z2, h2, z2, z2, z2, x2, mod, lw["w_out_a"], lw["w_out_b"], lw["w_out"], final_g.reshape(1, d))


def _to_chains(x, heads):
    b, t, _ = x.shape
    nt = b // CHAIN_BATCH
    x = x.reshape(nt, CHAIN_BATCH, t, heads, HEAD)
    return x.transpose(0, 2, 4, 1, 3).reshape(nt, t, HEAD, CHAIN_BATCH * heads)


def _from_chains(x, heads):
    nt, t = x.shape[0], x.shape[1]
    x = x.reshape(nt, t, HEAD, CHAIN_BATCH, heads)
    return x.transpose(0, 3, 1, 4, 2).reshape(nt * CHAIN_BATCH, t, heads * HEAD)


def _chain_const(p, heads):
    t = p.reshape(heads, HEAD).T
    return jnp.broadcast_to(t[:, None, :], (HEAD, CHAIN_BATCH, heads)).reshape(HEAD, CHAIN_BATCH * heads)


def _state_to_chains(s):
    b, h = s.shape[0], s.shape[1]
    nt = b // CHAIN_BATCH
    s = s.reshape(nt, CHAIN_BATCH, h, HEAD, HEAD)
    return s.transpose(0, 4, 3, 1, 2).reshape(nt, HEAD, HEAD, CHAIN_BATCH * h)


def _state_from_chains(s, heads):
    nt = s.shape[0]
    s = s.reshape(nt, HEAD, HEAD, CHAIN_BATCH, heads)
    return s.transpose(0, 3, 4, 2, 1).reshape(nt * CHAIN_BATCH, heads, HEAD, HEAD)


def _block_diag(w, per):
    h, n, _ = w.shape
    g = h // per
    eye = jnp.eye(per, dtype=w.dtype)
    wg = w.reshape(g, per, n, n)
    return (wg[:, :, :, None, :] * eye[None, :, None, :, None]).reshape(g, per * n, per * n)


def _layer_weights(l, n_layers, heads, w_in, mu_rkv, dec_w0, dec_w1, dec_w2, iclr_w0, iclr_w1, iclr_w2,
                   vres_w0, vres_w1, vres_w2, k_k, k_a, r_k, lnx_g, lnx_b, w_out_a, conv_w, conv_b, gr_w,
                   gr_b, gi_w, gi_b, lru_lambda, w_out_b, w_out):
    d = w_in.shape[1]
    w_a = heads * HEAD
    r_dec, r_iclr, r_vres = dec_w1.shape[-1], iclr_w1.shape[-1], vres_w1.shape[-1]
    assert (r_dec, r_iclr, r_vres) == (96, 96, 64), "LoRA column layout assumes ranks 96/96/64"
    has_vres = l > 0
    vw1 = vres_w1[l - 1] if has_vres else jnp.zeros((d, r_vres), F32)
    used = 2 * r_dec + 2 * r_iclr + r_vres
    w_ext = jnp.concatenate(
        [w_in[l], dec_w1[l, 0], dec_w1[l, 1], iclr_w1[l, 0], iclr_w1[l, 1], vw1,
         jnp.zeros((d, LORA_COLS - used), F32)], axis=1).astype(BF16)
    n_ext = w_ext.shape[1]
    tn = _col_tile(n_ext, TN_IN)
    w_ext = w_ext.reshape(d, n_ext // tn, tn).transpose(1, 0, 2)
    dec_w2p = jnp.zeros((2, 256, w_a), F32)
    iclr_w2p = jnp.zeros((2, 256, w_a), F32)
    for dd in range(2):
        dec_w2p = dec_w2p.at[dd, r_dec * dd:r_dec * (dd + 1)].set(dec_w2[l, dd])
        o = 2 * r_dec - 128 + r_iclr * dd
        iclr_w2p = iclr_w2p.at[dd, o:o + r_iclr].set(iclr_w2[l, dd])
    lw = dict(
        w_ext=w_ext, mu=mu_rkv[l], dec_w0=dec_w0[l], dec_w2p=dec_w2p.astype(BF16), iclr_w0=iclr_w0[l],
        iclr_w2p=iclr_w2p.astype(BF16),
        kk_c=_chain_const(k_k[l], heads), ka_c=_chain_const(k_a[l], heads),
        rk_c=_chain_const(r_k[l].reshape(-1), heads), g_c=_chain_const(lnx_g[l], heads),
        b_c=_chain_const(lnx_b[l], heads),
        conv_w=conv_w[l], conv_b=conv_b[l].reshape(1, -1),
        gr_bd=jnp.stack([_block_diag(gr_w[l, dd], CB_LRU // gr_w.shape[-1]) for dd in range(2)]).astype(BF16),
        gi_bd=jnp.stack([_block_diag(gi_w[l, dd], CB_LRU // gi_w.shape[-1]) for dd in range(2)]).astype(BF16),
        gr_b=gr_b[l], gi_b=gi_b[l], lam=lru_lambda[l],
        w_out_a=w_out_a[l].astype(BF16), w_out_b=w_out_b[l].astype(BF16), w_out=w_out[l].astype(BF16),
    )
    if has_vres:
        vres_w2p = jnp.zeros((128, w_a), F32).at[0:r_vres].set(vres_w2[l - 1])
        lw.update(vres_w0=vres_w0[l - 1].reshape(1, -1), vres_w2p=vres_w2p.astype(BF16))
    return lw


def _trunk(x, mods, mod_row_in, mod_row_out, tm_in, tm_out, grid_w, rwkv_init, lru_init, layers, final_g,
           heads, w_b):
    b, t, d = x.shape
    w_a = heads * HEAD
    n_layers = len(layers)
    x2 = x.reshape(b * t, d)
    v_first = None
    s_out, h_out = [], []
    for l, lw in enumerate(layers):
        n_ext = lw["w_ext"].shape[0] * lw["w_ext"].shape[2]
        z2 = _inproj(x2, mods[l], lw["norm_g"], lw["w_ext"], tm_in, mod_row_in)
        z3 = z2.reshape(b, t, n_ext)
        r, k, v, w_f, w_bk, a_f, a_bk = _prep(z3, v_first, lw, grid_w, w_a, n_ext - LORA_COLS)
        if v_first is None:
            v_first = v
        rc, kc, vc = (_to_chains(q, heads) for q in (r, k, v))
        afc, abc = _to_chains(a_f, heads), _to_chains(a_bk, heads)
        y_f, s_f = _scan(rc, kc, vc, _to_chains(w_f, heads), afc, lw["kk_c"], lw["ka_c"], rwkv_init[l][0], False)
        y_b, s_b = _scan(rc, kc, vc, _to_chains(w_bk, heads), abc, lw["kk_c"], lw["ka_c"], rwkv_init[l][1], True)
        y_rwkv = _from_chains(
            _gn(y_f, y_b, rc, kc, vc, afc, abc, lw["ka_c"], lw["rk_c"], lw["g_c"], lw["b_c"]), heads)
        h_sum, h_fin = _lru(z3, lru_init[l], lw, w_a, w_b)
        x2 = _out(y_rwkv.reshape(b * t, w_a), z2, h_sum.reshape(b * t, w_b), x2, mods[l], lw, final_g,
                  tm_out, mod_row_out, l == n_layers - 1, w_a, w_b)
        s_out.append((s_f, s_b))
        h_out.append(h_fin)
    return x2.reshape(b, t, d), s_out, h_out


def kernel(x_prompt, x_sample, c, state_rwkv, state_lru, c_ctx, norm_g, ada_w, ada_b, w_in, mu_rkv, dec_w0,
           dec_w1, dec_w2, iclr_w0, iclr_w1, iclr_w2, vres_w0, vres_w1, vres_w2, k_k, k_a, r_k, lnx_g, lnx_b,
           w_out_a, conv_w, conv_b, gr_w, gr_b, gi_w, gi_b, lru_lambda, w_out_b, w_out, final_g):
    n_layers, d = norm_g.shape
    bp, tp, _ = x_prompt.shape
    bs, ts, _ = x_sample.shape
    heads = r_k.shape[1]
    w_b = conv_w.shape[-1]
    grid_w = int(round(ts ** 0.5))
    assert grid_w * grid_w == ts and heads * CHAIN_BATCH == LANES
    assert bp % CHAIN_BATCH == 0 and bs % CHAIN_BATCH == 0

    layers = []
    for l in range(n_layers):
        lw = _layer_weights(l, n_layers, heads, w_in, mu_rkv, dec_w0, dec_w1, dec_w2, iclr_w0, iclr_w1,
                            iclr_w2, vres_w0, vres_w1, vres_w2, k_k, k_a, r_k, lnx_g, lnx_b, w_out_a, conv_w,
                            conv_b, gr_w, gr_b, gi_w, gi_b, lru_lambda, w_out_b, w_out)
        lw["norm_g"] = norm_g[l]
        layers.append(lw)

    ctx_row = bs
    n_rows = -(-(bs + 1) // SUBLANES) * SUBLANES
    cond = jnp.zeros((n_rows, d), F32).at[0:bs].set(c).at[ctx_row].set(c_ctx)
    mod = _ada_mod(cond, ada_w, ada_b).reshape(n_layers, n_rows, 3, d)
    mods = [mod[l] for l in range(n_layers)]

    zero_s = jnp.zeros((bp // CHAIN_BATCH, HEAD, HEAD, LANES), F32)
    tm_p = min(TM_IN, bp * tp)
    tm_po = min(TM_OUT, bp * tp)
    x_ctx, s_ctx, h_ctx = _trunk(
        x_prompt, mods, lambda i: ctx_row, lambda i: ctx_row, tm_p, tm_po, 0,
        [(zero_s, zero_s)] * n_layers, [jnp.zeros((bp, 2, w_b), F32)] * n_layers, layers, final_g, heads, w_b)

    tm_s = min(TM_IN, ts)
    tm_so = min(TM_OUT, ts)
    assert ts % tm_s == 0 and ts % tm_so == 0
    rwkv_init = [tuple(_state_to_chains(state_rwkv[:, l, dd]) for dd in range(2)) for l in range(n_layers)]
    x_lat, _, _ = _trunk(
        x_sample, mods, lambda i: i // (ts // tm_s), lambda i: i // (ts // tm_so), tm_s, tm_so, grid_w,
        rwkv_init, [state_lru[:, l] for l in range(n_layers)], layers, final_g, heads, w_b)

    new_rwkv = jnp.stack(
        [jnp.stack([_state_from_chains(s_ctx[l][dd], heads) for dd in range(2)], axis=1)
         for l in range(n_layers)], axis=1)
    new_lru = jnp.stack(h_ctx, axis=1)
    return x_ctx, x_lat, new_rwkv, new_lru
```

```python
import functools
import math

import jax
import jax.numpy as jnp
from jax import lax
from jax.experimental import pallas as pl
from jax.experimental.pallas import tpu as pltpu

F32 = jnp.float32
BF16 = jnp.bfloat16

LANES = 128
SUBLANES = 8
HEAD = 64
CHAIN_BATCH = 8
LORA_COLS = 512
RMS_EPS = 1e-6
GN_EPS = 64e-5
LRU_C = 8.0
KK_EPS = 1e-12
DECAY_SCALE = math.exp(-0.5)
VMEM_LIMIT = 56 * 1024 * 1024

TM_IN = 1024
TN_IN = 1536
ROWS_NORM = 16
TM_OUT = 256
TT_PREP = 512
ROWS_PREP = 64
CW_PREP = 256
TB_SCAN = 32
ROWS_SCAN = 32
TB_GN = 64
CB_LRU = 256
CH_LRU = 512


def _params(*sem):
    return pltpu.CompilerParams(dimension_semantics=sem, vmem_limit_bytes=VMEM_LIMIT)


def _cost(flops, transcendentals, *arrays):
    nbytes = sum(math.prod(a.shape) * jnp.dtype(a.dtype).itemsize for a in arrays)
    return pl.CostEstimate(flops=int(flops), transcendentals=int(transcendentals), bytes_accessed=int(nbytes))


def _softplus(x):
    return jnp.maximum(x, 0.0) + jnp.log1p(jnp.exp(-jnp.abs(x)))


def _sigmoid(x):
    return 0.5 * jnp.tanh(0.5 * x) + 0.5


def _silu(x):
    return x * _sigmoid(x)


def _ada_kernel(c_ref, w_ref, b_ref, o_ref):
    h = _silu(c_ref[...]).astype(BF16)
    o_ref[0] = jnp.dot(h, w_ref[0].astype(BF16), preferred_element_type=F32) + b_ref[0]


def _ada_mod(cond, ada_w, ada_b):
    n_l, d, n = ada_w.shape
    bc = cond.shape[0]
    tn = math.gcd(n, 512)
    return pl.pallas_call(
        _ada_kernel,
        grid=(n_l, n // tn),
        in_specs=[
            pl.BlockSpec((bc, d), lambda l, j: (0, 0)),
            pl.BlockSpec((1, d, tn), lambda l, j: (l, 0, j)),
            pl.BlockSpec((1, 1, tn), lambda l, j: (l, 0, j)),
        ],
        out_specs=pl.BlockSpec((1, bc, tn), lambda l, j: (l, 0, j)),
        out_shape=jax.ShapeDtypeStruct((n_l, bc, n), F32),
        compiler_params=_params("arbitrary", "arbitrary"),
        name="ada",
    )(cond, ada_w, ada_b.reshape(n_l, 1, n))


def _inproj_kernel(x_ref, mod_ref, g_ref, w_ref, o_ref, xm_ref):
    @pl.when(pl.program_id(1) == 0)
    def _():
        gain = g_ref[...]
        scale = 1.0 + mod_ref[0, 1:2, :]
        shift = mod_ref[0, 0:1, :]

        def rows_body(c, carry):
            rows = pl.ds(pl.multiple_of(c * ROWS_NORM, ROWS_NORM), ROWS_NORM)
            x = x_ref[rows, :]
            y = x * lax.rsqrt(jnp.mean(x * x, axis=-1, keepdims=True) + RMS_EPS) * gain
            xm_ref[rows, :] = (y * scale + shift).astype(BF16)
            return carry

        lax.fori_loop(0, x_ref.shape[0] // ROWS_NORM, rows_body, 0, unroll=8)

    o_ref[...] = jnp.dot(xm_ref[...], w_ref[0], preferred_element_type=F32)


def _col_tile(n, cap):
    return 256 * max(q for q in range(1, cap // 256 + 1) if (n // 256) % q == 0)


def _inproj(x2, mod, norm_g, w_tiles, tm, mod_row):
    m, d = x2.shape
    nj, _, tn = w_tiles.shape
    return pl.pallas_call(
        _inproj_kernel,
        grid=(m // tm, nj),
        in_specs=[
            pl.BlockSpec((tm, d), lambda i, j: (i, 0)),
            pl.BlockSpec((1, 3, d), lambda i, j: (mod_row(i), 0, 0)),
            pl.BlockSpec((1, d), lambda i, j: (0, 0)),
            pl.BlockSpec((1, d, tn), lambda i, j: (j, 0, 0)),
        ],
        out_specs=pl.BlockSpec((tm, tn), lambda i, j: (i, j)),
        out_shape=jax.ShapeDtypeStruct((m, nj * tn), F32),
        scratch_shapes=[pltpu.VMEM((tm, d), BF16)],
        compiler_params=_params("arbitrary", "arbitrary"),
        cost_estimate=_cost(2 * m * d * nj * tn, m, x2, w_tiles, jax.ShapeDtypeStruct((m, nj * tn), F32)),
        name="inproj",
    )(x2, mod, norm_g.reshape(1, d), w_tiles)


def _prep_kernel(*refs, tt, cw, grid_w, n_t, has_vres):
    it = iter(refs)
    zr, zk, zv = next(it), next(it), next(it)
    halos = [next(it) for _ in range(6)] if grid_w else [None] * 6
    lo_ref = next(it)
    vfirst = next(it) if has_vres else None
    mu, dw0, dw2, iw0, iw2 = (next(it) for _ in range(5))
    vw0, vw2 = (next(it), next(it)) if has_vres else (None, None)
    r_o, k_o, v_o, wf_o, wb_o, af_o, ab_o = (next(it) for _ in range(7))
    td_s, li_s, lv_s, er_s, ek_s, ev_s = (next(it) for _ in range(6))

    @pl.when(pl.program_id(2) == 0)
    def _():
        td_s[...] = jnp.tanh(lo_ref[0, :, 0:256]).astype(BF16)
        li_s[...] = lo_ref[0, :, 128:384].astype(BF16)
        lv_s[...] = lo_ref[0, :, 384:512].astype(BF16)

    pad = grid_w if grid_w else SUBLANES
    for e_s, z_ref, hp, hn in ((er_s, zr, halos[0], halos[1]), (ek_s, zk, halos[2], halos[3]),
                               (ev_s, zv, halos[4], halos[5])):
        if grid_w:
            i = pl.program_id(1)
            e_s[0:pad, :] = jnp.where(i > 0, hp[0], 0.0)
            e_s[pad + tt:2 * pad + tt, :] = jnp.where(i < n_t - 1, hn[0], 0.0)
        else:
            e_s[0:pad, :] = jnp.zeros((pad, cw), F32)
            e_s[pad + tt:2 * pad + tt, :] = jnp.zeros((pad, cw), F32)
        e_s[pad:pad + tt, :] = z_ref[0]

    rc = min(tt, ROWS_PREP)
    col = lax.rem(lax.broadcasted_iota(jnp.int32, (rc, cw), 0), grid_w) if grid_w else None

    def mix(e_s, c0, j):
        z = e_s[pad + c0:pad + c0 + rc, :]
        before = e_s[pad + c0 - 1:pad + c0 - 1 + rc, :]
        after = e_s[pad + c0 + 1:pad + c0 + 1 + rc, :]
        if grid_w:
            up = e_s[c0:c0 + rc, :]
            down = e_s[2 * pad + c0:2 * pad + c0 + rc, :]
            left = jnp.where(col == 0, 0.0, before)
            right = jnp.where(col == grid_w - 1, 0.0, after)
            sh = 0.25 * (up + down + left + right)
        else:
            sh = 0.5 * (before + after)
        return z + (sh - z) * mu[j:j + 1, :]

    for c0 in range(0, tt, rc):
        rows = pl.ds(c0, rc)
        r_o[0, rows, :] = mix(er_s, c0, 0)
        k_o[0, rows, :] = mix(ek_s, c0, 1)
        v = mix(ev_s, c0, 2)
        if has_vres:
            gate = _sigmoid(vw0[...] + jnp.dot(lv_s[rows, :], vw2[...], preferred_element_type=F32))
            v = v + (vfirst[0, rows, :] - v) * gate
        v_o[0, rows, :] = v
        td = td_s[rows, :]
        li = li_s[rows, :]
        for d, (w_o, a_o) in enumerate(((wf_o, af_o), (wb_o, ab_o))):
            xd = dw0[d:d + 1, :] + jnp.dot(td, dw2[d], preferred_element_type=F32)
            w_o[0, rows, :] = -DECAY_SCALE * _sigmoid(xd)
            a_o[0, rows, :] = _sigmoid(iw0[d:d + 1, :] + jnp.dot(li, iw2[d], preferred_element_type=F32))


def _prep(z3, v_first, lw, grid_w, w_a, n_main):
    b, t, _ = z3.shape
    cw = CW_PREP
    nc = w_a // cw
    has_vres = v_first is not None
    if grid_w:
        tt = min(TT_PREP, t)
        assert min(tt, ROWS_PREP) % grid_w == 0 and grid_w % SUBLANES == 0 and t % tt == 0
    else:
        tt = t
    n_t = t // tt
    blk = lambda off: pl.BlockSpec((1, tt, cw), lambda bb, i, c, off=off: (bb, i, off + c))
    in_specs = [blk(0), blk(nc), blk(2 * nc)]
    args = [z3, z3, z3]
    if grid_w:
        per, last = tt // grid_w, t // grid_w - 1
        for off in (0, nc, 2 * nc):
            in_specs.append(pl.BlockSpec(
                (1, grid_w, cw),
                lambda bb, i, c, off=off: (bb, jnp.maximum(i * per - 1, 0), off + c)))
            in_specs.append(pl.BlockSpec(
                (1, grid_w, cw),
                lambda bb, i, c, off=off: (bb, jnp.minimum((i + 1) * per, last), off + c)))
            args += [z3, z3]
    in_specs.append(pl.BlockSpec((1, tt, LORA_COLS), lambda bb, i, c: (bb, i, n_main // LORA_COLS)))
    args.append(z3)
    if has_vres:
        in_specs.append(blk(0))
        args.append(v_first)
    wcol = lambda shape: pl.BlockSpec(shape, lambda bb, i, c: (0,) * (len(shape) - 1) + (c,))
    in_specs += [wcol((3, cw)), wcol((2, cw)), wcol((2, 256, cw)), wcol((2, cw)), wcol((2, 256, cw))]
    args += [lw["mu"], lw["dec_w0"], lw["dec_w2p"], lw["iclr_w0"], lw["iclr_w2p"]]
    if has_vres:
        in_specs += [wcol((1, cw)), wcol((128, cw))]
        args += [lw["vres_w0"], lw["vres_w2p"]]
    out = jax.ShapeDtypeStruct((b, t, w_a), F32)
    return pl.pallas_call(
        functools.partial(_prep_kernel, tt=tt, cw=cw, grid_w=grid_w, n_t=n_t, has_vres=has_vres),
        grid=(b, n_t, nc),
        in_specs=in_specs,
        out_specs=[blk(0)] * 7,
        out_shape=[out] * 7,
        scratch_shapes=[pltpu.VMEM((tt, 256), BF16), pltpu.VMEM((tt, 256), BF16), pltpu.VMEM((tt, 128), BF16)]
        + [pltpu.VMEM((tt + 2 * (grid_w if grid_w else SUBLANES), cw), F32)] * 3,
        compiler_params=_params("arbitrary", "arbitrary", "arbitrary"),
        cost_estimate=_cost(b * t * w_a * (2 * 5 * 256 + 60), 6 * b * t * w_a, *[out] * 11),
        name="prep",
    )(*args)


def _scan_kernel(r_ref, k_ref, v_ref, lw_ref, a_ref, kkc_ref, kac_ref, rkc_ref, s0_ref, y_ref, bo_ref,
                 sfin_ref, s_scr, al_s, be_s, kt_s, rt_s, *, tb, reverse):
    i = pl.program_id(1)

    @pl.when(i == 0)
    def _():
        s_scr[...] = s0_ref[0]

    off = 1 if reverse else 0
    al_s[0 if reverse else tb] = jnp.zeros((HEAD, LANES), F32)

    def prep_step(s, carry):
        lcum, wprev = carry
        t = (tb - 1 - s) if reverse else s
        k_t, a_t, r_t = k_ref[0, t], a_ref[0, t], r_ref[0, t]
        kkr = k_t * kkc_ref[...]
        nrm = jnp.sqrt(jnp.sum(kkr * kkr, axis=0, keepdims=True))
        kk = kkr / jnp.maximum(nrm, KK_EPS)
        al_s[t + off] = wprev * kk
        lcum = lcum + lw_ref[0, t]
        wcum = jnp.exp(lcum)
        inv = jnp.exp(-lcum)
        kd = k_t * (1.0 + (a_t - 1.0) * kac_ref[...])
        be_s[t] = (kk * a_t) * inv
        kt_s[t] = kd * inv
        rt_s[t] = wcum * r_t
        bo = jnp.sum((r_t * kd) * rkc_ref[...], axis=0, keepdims=True)
        bo_ref[0, t] = jnp.broadcast_to(bo, (SUBLANES, LANES))
        return lcum, wcum

    _, w_last = lax.fori_loop(0, tb, prep_step, (jnp.zeros((HEAD, LANES), F32), jnp.ones((HEAD, LANES), F32)),
                              unroll=4)

    groups = [pl.ds(g * ROWS_SCAN, ROWS_SCAN) for g in range(HEAD // ROWS_SCAN)]

    def s_dot_al(rows, slot):
        acc = [None, None]
        for kx in range(HEAD):
            term = s_scr[kx, rows, :] * al_s[slot, pl.ds(kx, 1), :]
            acc[kx % 2] = term if acc[kx % 2] is None else acc[kx % 2] + term
        return acc[0] + acc[1]

    def step(s, sal_all):
        t = (tb - 1 - s) if reverse else s
        nxt = t if reverse else t + 1
        sal_next = []
        for rows, sal in zip(groups, sal_all):
            vv = v_ref[0, t, rows, :]
            yac = acc = None
            for kx in range(HEAD):
                kr = pl.ds(kx, 1)
                sn = s_scr[kx, rows, :] - sal * be_s[t, kr, :] + vv * kt_s[t, kr, :]
                s_scr[kx, rows, :] = sn
                ty = sn * rt_s[t, kr, :]
                ta = sn * al_s[nxt, kr, :]
                yac = ty if yac is None else yac + ty
                acc = ta if acc is None else acc + ta
            y_ref[0, t, rows, :] = yac
            sal_next.append(acc)
        return tuple(sal_next)

    first = (tb - 1 + off) if reverse else 0
    lax.fori_loop(0, tb, step, tuple(s_dot_al(rows, first) for rows in groups))

    wl_s = be_s.at[0]
    wl_s[...] = w_last
    for kx in range(HEAD):
        s_scr[kx] = s_scr[kx] * wl_s[pl.ds(kx, 1), :]

    @pl.when(i == pl.num_programs(1) - 1)
    def _():
        sfin_ref[0] = s_scr[...]


def _scan(r, k, v, lw, a, kk_c, ka_c, rk_c, s0, reverse):
    nt, t = r.shape[0], r.shape[1]
    tb = min(TB_SCAN, t)
    nb = t // tb
    tmap = (lambda c, i: (c, nb - 1 - i, 0, 0)) if reverse else (lambda c, i: (c, i, 0, 0))
    seq = pl.BlockSpec((1, tb, HEAD, LANES), tmap)
    row = pl.BlockSpec((1, tb, SUBLANES, LANES), tmap)
    cst = pl.BlockSpec((HEAD, LANES), lambda c, i: (0, 0))
    st = pl.BlockSpec((1, HEAD, HEAD, LANES), lambda c, i: (c, 0, 0, 0))
    return pl.pallas_call(
        functools.partial(_scan_kernel, tb=tb, reverse=reverse),
        grid=(nt, nb),
        in_specs=[seq] * 5 + [cst, cst, cst, st],
        out_specs=[seq, row, st],
        out_shape=[jax.ShapeDtypeStruct(r.shape, F32), jax.ShapeDtypeStruct((nt, t, SUBLANES, LANES), F32),
                   jax.ShapeDtypeStruct(s0.shape, F32)],
        scratch_shapes=[pltpu.VMEM((HEAD, HEAD, LANES), F32), pltpu.VMEM((tb + 1, HEAD, LANES), F32)]
        + [pltpu.VMEM((tb, HEAD, LANES), F32)] * 3,
        compiler_params=_params("arbitrary", "arbitrary"),
        cost_estimate=_cost(8 * r.size * HEAD, 4 * r.size, r, k, v, lw, a, r, s0, s0),
        name="scan_bwd" if reverse else "scan_fwd",
    )(r, k, v, lw, a, kk_c, ka_c, rk_c, s0)


def _gn_kernel(yf_ref, yb_ref, v_ref, bof_ref, bob_ref, g_ref, b_ref, o_ref):
    y = yf_ref[0] + yb_ref[0]
    mean = jnp.mean(y, axis=1, keepdims=True)
    dlt = y - mean
    var = jnp.mean(dlt * dlt, axis=1, keepdims=True)
    yn = dlt * lax.rsqrt(var + GN_EPS) * g_ref[...][None] + b_ref[...][None]
    bo = bof_ref[0, :, 0:1, :] + bob_ref[0, :, 0:1, :]
    o_ref[0] = yn + bo * v_ref[0]


def _gn(yf, yb, v, bo_f, bo_b, g_c, b_c):
    nt, t = v.shape[0], v.shape[1]
    tb = min(TB_GN, t)
    seq = pl.BlockSpec((1, tb, HEAD, LANES), lambda c, i: (c, i, 0, 0))
    row = pl.BlockSpec((1, tb, SUBLANES, LANES), lambda c, i: (c, i, 0, 0))
    cst = pl.BlockSpec((HEAD, LANES), lambda c, i: (0, 0))
    return pl.pallas_call(
        _gn_kernel,
        grid=(nt, t // tb),
        in_specs=[seq] * 3 + [row] * 2 + [cst] * 2,
        out_specs=seq,
        out_shape=jax.ShapeDtypeStruct(v.shape, F32),
        compiler_params=_params("arbitrary", "arbitrary"),
        cost_estimate=_cost(12 * v.size, v.size // HEAD, yf, yb, v, v, bo_f, bo_b),
        name="gn",
    )(yf, yb, v, bo_f, bo_b, g_c, b_c)


def _lru_kernel(x_ref, h0_ref, cw_ref, cb_ref, wr_ref, br_ref, wi_ref, bi_ref, lam_ref, o_ref, hfin_ref,
                xpad, af, uf, ab, ub, *, t_len, ch):
    cbk = x_ref.shape[-1]
    pad = SUBLANES
    xpad[0:pad, :] = jnp.zeros((pad, cbk), F32)
    xpad[t_len + pad:t_len + 2 * pad, :] = jnp.zeros((pad, cbk), F32)
    xpad[pad:t_len + pad, :] = x_ref[0]
    cw = cw_ref[...]
    for c0 in range(0, t_len, ch):
        xc = cb_ref[...] + xpad[c0 + pad - 2:c0 + pad - 2 + ch, :] * cw[0:1, :]
        for j in range(1, 4):
            xc = xc + xpad[c0 + pad - 2 + j:c0 + pad - 2 + j + ch, :] * cw[j:j + 1, :]
        xcb = xc.astype(BF16)
        for d, (a_s, u_s) in enumerate(((af, uf), (ab, ub))):
            rg = _sigmoid(jnp.dot(xcb, wr_ref[d, 0], preferred_element_type=F32) + br_ref[d:d + 1, :])
            ig = _sigmoid(jnp.dot(xcb, wi_ref[d, 0], preferred_element_type=F32) + bi_ref[d:d + 1, :])
            log_a = (-LRU_C * _softplus(-lam_ref[d:d + 1, :])) * rg
            a = jnp.exp(log_a)
            a_s[c0:c0 + ch, :] = a
            s2 = 1.0 - a * a
            root = jnp.where(s2 > 0.0, s2 * lax.rsqrt(s2), 0.0)
            u_s[c0:c0 + ch, :] = root * (ig * xc)

    rowi = lax.broadcasted_iota(jnp.int32, (SUBLANES, cbk), 0)

    def tile_scan(a, u, h_in, backward):
        for s in (1, 2, 4):
            sh = SUBLANES - s if backward else s
            a_s, u_s = pltpu.roll(a, sh, 0), pltpu.roll(u, sh, 0)
            m = (rowi < SUBLANES - s) if backward else (rowi >= s)
            u = jnp.where(m, u + a * u_s, u)
            a = jnp.where(m, a * a_s, a)
        h = a * h_in + u
        return h, (h[0:1, :] if backward else h[SUBLANES - 1:SUBLANES, :])

    def body(i, carry):
        hf, hb = carry
        of = pl.multiple_of(i * SUBLANES, SUBLANES)
        h, hf = tile_scan(af[pl.ds(of, SUBLANES), :], uf[pl.ds(of, SUBLANES), :], hf, False)
        af[pl.ds(of, SUBLANES), :] = h
        ob = pl.multiple_of(t_len - SUBLANES - i * SUBLANES, SUBLANES)
        h, hb = tile_scan(ab[pl.ds(ob, SUBLANES), :], ub[pl.ds(ob, SUBLANES), :], hb, True)
        ab[pl.ds(ob, SUBLANES), :] = h
        return hf, hb

    hf, hb = lax.fori_loop(0, t_len // SUBLANES, body, (h0_ref[0, 0:1, :], h0_ref[0, 1:2, :]), unroll=4)
    o_ref[0] = af[...] + ab[...]
    hfin_ref[0, 0:1, :] = hf
    hfin_ref[0, 1:2, :] = hb


def _lru(z3, h0, lw, w_a, w_b):
    b, t, _ = z3.shape
    cbk = CB_LRU
    ncb = w_b // cbk
    off = 4 * w_a // cbk
    ch = min(CH_LRU, t)
    wcol = lambda rows: pl.BlockSpec((rows, cbk), lambda bb, c: (0, c))
    wgate = pl.BlockSpec((2, 1, cbk, cbk), lambda bb, c: (0, c, 0, 0))
    st = pl.BlockSpec((1, 2, cbk), lambda bb, c: (bb, 0, c))
    return pl.pallas_call(
        functools.partial(_lru_kernel, t_len=t, ch=ch),
        grid=(b, ncb),
        in_specs=[pl.BlockSpec((1, t, cbk), lambda bb, c: (bb, 0, off + c)), st,
                  wcol(4), wcol(1), wgate, wcol(2), wgate, wcol(2), wcol(2)],
        out_specs=[pl.BlockSpec((1, t, cbk), lambda bb, c: (bb, 0, c)), st],
        out_shape=[jax.ShapeDtypeStruct((b, t, w_b), F32), jax.ShapeDtypeStruct((b, 2, w_b), F32)],
        scratch_shapes=[pltpu.VMEM((t + 2 * SUBLANES, cbk), F32)] + [pltpu.VMEM((t, cbk), F32)] * 4,
        compiler_params=_params("arbitrary", "arbitrary"),
        cost_estimate=_cost(b * t * w_b * (8 * cbk + 150), 8 * b * t * w_b,
                            *[jax.ShapeDtypeStruct((b, t, w_b), F32)] * 2),
        name="lru",
    )(z3, h0, lw["conv_w"], lw["conv_b"], lw["gr_bd"], lw["gr_b"], lw["gi_bd"], lw["gi_b"], lw["lam"])


def _out_kernel(y_ref, ga_ref, h_ref, gb_ref, ma_ref, mb_ref, x_ref, mod_ref, woa_ref, wob_ref, wo_ref,
                fg_ref, o_ref, *, final):
    ya = jnp.dot((y_ref[...] * _silu(ga_ref[...])).astype(BF16), woa_ref[...], preferred_element_type=F32)
    yb = jnp.dot((h_ref[...] * _silu(gb_ref[...])).astype(BF16), wob_ref[...], preferred_element_type=F32)
    merged = _sigmoid(ma_ref[...]) * ya + _sigmoid(mb_ref[...]) * yb
    out = jnp.dot(merged.astype(BF16), wo_ref[...], preferred_element_type=F32)
    xn = x_ref[...] + mod_ref[0, 2:3, :] * out
    if final:
        xn = xn * lax.rsqrt(jnp.mean(xn * xn, axis=-1, keepdims=True) + RMS_EPS) * fg_ref[...]
    o_ref[...] = xn


def _out(y2, z2, h2, x2, mod, lw, final_g, tm, mod_row, final, w_a, w_b):
    m, d = x2.shape
    once = dict(pipeline_mode=pl.Buffered(1))
    zoff_ga = 3 * w_a // w_a
    zoff_gb = (4 * w_a + w_b) // w_b
    zoff_ma = (4 * w_a + 2 * w_b) // d
    return pl.pallas_call(
        functools.partial(_out_kernel, final=final),
        grid=(m // tm,),
        in_specs=[
            pl.BlockSpec((tm, w_a), lambda i: (i, 0)),
            pl.BlockSpec((tm, w_a), lambda i: (i, zoff_ga)),
            pl.BlockSpec((tm, w_b), lambda i: (i, 0)),
            pl.BlockSpec((tm, w_b), lambda i: (i, zoff_gb)),
            pl.BlockSpec((tm, d), lambda i: (i, zoff_ma)),
            pl.BlockSpec((tm, d), lambda i: (i, zoff_ma + 1)),
            pl.BlockSpec((tm, d), lambda i: (i, 0)),
            pl.BlockSpec((1, 3, d), lambda i: (mod_row(i), 0, 0)),
            pl.BlockSpec((w_a, d), lambda i: (0, 0), **once),
            pl.BlockSpec((w_b, d), lambda i: (0, 0), **once),
            pl.BlockSpec((d, d), lambda i: (0, 0), **once),
            pl.BlockSpec((1, d), lambda i: (0, 0)),
        ],
        out_specs=pl.BlockSpec((tm, d), lambda i: (i, 0)),
        out_shape=jax.ShapeDtypeStruct((m, d), F32),
        compiler_params=_params("arbitrary"),
        cost_estimate=_cost(2 * m * d * (w_a + w_b + d), 4 * m * d, y2, y2, h2, h2, x2, x2, x2, x2,
                            lw["w_out_a"], lw["w_out_b"], lw["w_out"]),
        name="out_final" if final else "out",
    )(y2, z2, h2, z2, z2, z2, x2, mod, lw["w_out_a"], lw["w_out_b"], lw["w_out"], final_g.reshape(1, d))


def _to_chains(x, heads):
    b, t, _ = x.shape
    nt = b // CHAIN_BATCH
    y = jnp.swapaxes(x.reshape(nt, CHAIN_BATCH, t, heads * HEAD), 1, 2)
    return jnp.swapaxes(y.reshape(nt, t, CHAIN_BATCH * heads, HEAD), 2, 3)


def _from_chains(x, heads):
    nt, t = x.shape[0], x.shape[1]
    y = jnp.swapaxes(x, 2, 3).reshape(nt, t, CHAIN_BATCH, heads * HEAD)
    return jnp.swapaxes(y, 1, 2).reshape(nt * CHAIN_BATCH, t, heads * HEAD)


def _chain_const(p, heads):
    t = p.reshape(heads, HEAD).T
    return jnp.broadcast_to(t[:, None, :], (HEAD, CHAIN_BATCH, heads)).reshape(HEAD, CHAIN_BATCH * heads)


def _state_to_chains(s):
    b, h = s.shape[0], s.shape[1]
    nt = b // CHAIN_BATCH
    s = s.reshape(nt, CHAIN_BATCH, h, HEAD, HEAD)
    return s.transpose(0, 4, 3, 1, 2).reshape(nt, HEAD, HEAD, CHAIN_BATCH * h)


def _state_from_chains(s, heads):
    nt = s.shape[0]
    s = s.reshape(nt, HEAD, HEAD, CHAIN_BATCH, heads)
    return s.transpose(0, 3, 4, 2, 1).reshape(nt * CHAIN_BATCH, heads, HEAD, HEAD)


def _block_diag(w, per):
    h, n, _ = w.shape
    g = h // per
    eye = jnp.eye(per, dtype=w.dtype)
    wg = w.reshape(g, per, n, n)
    return (wg[:, :, :, None, :] * eye[None, :, None, :, None]).reshape(g, per * n, per * n)


def _layer_weights(l, n_layers, heads, w_in, mu_rkv, dec_w0, dec_w1, dec_w2, iclr_w0, iclr_w1, iclr_w2,
                   vres_w0, vres_w1, vres_w2, k_k, k_a, r_k, lnx_g, lnx_b, w_out_a, conv_w, conv_b, gr_w,
                   gr_b, gi_w, gi_b, lru_lambda, w_out_b, w_out):
    d = w_in.shape[1]
    w_a = heads * HEAD
    r_dec, r_iclr, r_vres = dec_w1.shape[-1], iclr_w1.shape[-1], vres_w1.shape[-1]
    assert (r_dec, r_iclr, r_vres) == (96, 96, 64), "LoRA column layout assumes ranks 96/96/64"
    has_vres = l > 0
    vw1 = vres_w1[l - 1] if has_vres else jnp.zeros((d, r_vres), F32)
    used = 2 * r_dec + 2 * r_iclr + r_vres
    w_ext = jnp.concatenate(
        [w_in[l], dec_w1[l, 0], dec_w1[l, 1], iclr_w1[l, 0], iclr_w1[l, 1], vw1,
         jnp.zeros((d, LORA_COLS - used), F32)], axis=1).astype(BF16)
    n_ext = w_ext.shape[1]
    tn = _col_tile(n_ext, TN_IN)
    w_ext = w_ext.reshape(d, n_ext // tn, tn).transpose(1, 0, 2)
    dec_w2p = jnp.zeros((2, 256, w_a), F32)
    iclr_w2p = jnp.zeros((2, 256, w_a), F32)
    for dd in range(2):
        dec_w2p = dec_w2p.at[dd, r_dec * dd:r_dec * (dd + 1)].set(dec_w2[l, dd])
        o = 2 * r_dec - 128 + r_iclr * dd
        iclr_w2p = iclr_w2p.at[dd, o:o + r_iclr].set(iclr_w2[l, dd])
    lw = dict(
        w_ext=w_ext, mu=mu_rkv[l], dec_w0=dec_w0[l], dec_w2p=dec_w2p.astype(BF16), iclr_w0=iclr_w0[l],
        iclr_w2p=iclr_w2p.astype(BF16),
        kk_c=_chain_const(k_k[l], heads), ka_c=_chain_const(k_a[l], heads),
        rk_c=_chain_const(r_k[l].reshape(-1), heads), g_c=_chain_const(lnx_g[l], heads),
        b_c=_chain_const(lnx_b[l], heads),
        conv_w=conv_w[l], conv_b=conv_b[l].reshape(1, -1),
        gr_bd=jnp.stack([_block_diag(gr_w[l, dd], CB_LRU // gr_w.shape[-1]) for dd in range(2)]).astype(BF16),
        gi_bd=jnp.stack([_block_diag(gi_w[l, dd], CB_LRU // gi_w.shape[-1]) for dd in range(2)]).astype(BF16),
        gr_b=gr_b[l], gi_b=gi_b[l], lam=lru_lambda[l],
        w_out_a=w_out_a[l].astype(BF16), w_out_b=w_out_b[l].astype(BF16), w_out=w_out[l].astype(BF16),
    )
    if has_vres:
        vres_w2p = jnp.zeros((128, w_a), F32).at[0:r_vres].set(vres_w2[l - 1])
        lw.update(vres_w0=vres_w0[l - 1].reshape(1, -1), vres_w2p=vres_w2p.astype(BF16))
    return lw


def _trunk(x, mods, mod_row_in, mod_row_out, tm_in, tm_out, grid_w, rwkv_init, lru_init, layers, final_g,
           heads, w_b):
    b, t, d = x.shape
    w_a = heads * HEAD
    n_layers = len(layers)
    x2 = x.reshape(b * t, d)
    v_first = None
    s_out, h_out = [], []
    for l, lw in enumerate(layers):
        n_ext = lw["w_ext"].shape[0] * lw["w_ext"].shape[2]
        z2 = _inproj(x2, mods[l], lw["norm_g"], lw["w_ext"], tm_in, mod_row_in)
        z3 = z2.reshape(b, t, n_ext)
        r, k, v, lw_f, lw_bk, a_f, a_bk = _prep(z3, v_first, lw, grid_w, w_a, n_ext - LORA_COLS)
        if v_first is None:
            v_first = v
        h_sum, h_fin = _lru(z3, lru_init[l], lw, w_a, w_b)
        rc, kc, vc = (_to_chains(q, heads) for q in (r, k, v))
        consts = (lw["kk_c"], lw["ka_c"], lw["rk_c"])
        y_f, bo_f, s_f = _scan(rc, kc, vc, _to_chains(lw_f, heads), _to_chains(a_f, heads), *consts,
                               rwkv_init[l][0], False)
        y_b, bo_b, s_b = _scan(rc, kc, vc, _to_chains(lw_bk, heads), _to_chains(a_bk, heads), *consts,
                               rwkv_init[l][1], True)
        y_rwkv = _from_chains(_gn(y_f, y_b, vc, bo_f, bo_b, lw["g_c"], lw["b_c"]), heads)
        x2 = _out(y_rwkv.reshape(b * t, w_a), z2, h_sum.reshape(b * t, w_b), x2, mods[l], lw, final_g,
                  tm_out, mod_row_out, l == n_layers - 1, w_a, w_b)
        s_out.append((s_f, s_b))
        h_out.append(h_fin)
    return x2.reshape(b, t, d), s_out, h_out


def kernel(x_prompt, x_sample, c, state_rwkv, state_lru, c_ctx, norm_g, ada_w, ada_b, w_in, mu_rkv, dec_w0,
           dec_w1, dec_w2, iclr_w0, iclr_w1, iclr_w2, vres_w0, vres_w1, vres_w2, k_k, k_a, r_k, lnx_g, lnx_b,
           w_out_a, conv_w, conv_b, gr_w, gr_b, gi_w, gi_b, lru_lambda, w_out_b, w_out, final_g):
    n_layers, d = norm_g.shape
    bp, tp, _ = x_prompt.shape
    bs, ts, _ = x_sample.shape
    heads = r_k.shape[1]
    w_b = conv_w.shape[-1]
    grid_w = int(round(ts ** 0.5))
    assert grid_w * grid_w == ts and heads * CHAIN_BATCH == LANES
    assert bp % CHAIN_BATCH == 0 and bs % CHAIN_BATCH == 0

    layers = []
    for l in range(n_layers):
        lw = _layer_weights(l, n_layers, heads, w_in, mu_rkv, dec_w0, dec_w1, dec_w2, iclr_w0, iclr_w1,
                            iclr_w2, vres_w0, vres_w1, vres_w2, k_k, k_a, r_k, lnx_g, lnx_b, w_out_a, conv_w,
                            conv_b, gr_w, gr_b, gi_w, gi_b, lru_lambda, w_out_b, w_out)
        lw["norm_g"] = norm_g[l]
        layers.append(lw)

    ctx_row = bs
    n_rows = -(-(bs + 1) // SUBLANES) * SUBLANES
    cond = jnp.zeros((n_rows, d), F32).at[0:bs].set(c).at[ctx_row].set(c_ctx)
    mod = _ada_mod(cond, ada_w, ada_b).reshape(n_layers, n_rows, 3, d)
    mods = [mod[l] for l in range(n_layers)]

    zero_s = jnp.zeros((bp // CHAIN_BATCH, HEAD, HEAD, LANES), F32)
    tm_p = min(TM_IN, bp * tp)
    tm_po = min(TM_OUT, bp * tp)
    x_ctx, s_ctx, h_ctx = _trunk(
        x_prompt, mods, lambda i: ctx_row, lambda i: ctx_row, tm_p, tm_po, 0,
        [(zero_s, zero_s)] * n_layers, [jnp.zeros((bp, 2, w_b), F32)] * n_layers, layers, final_g, heads, w_b)

    tm_s = min(TM_IN, ts)
    tm_so = min(TM_OUT, ts)
    assert ts % tm_s == 0 and ts % tm_so == 0
    rwkv_init = [tuple(_state_to_chains(state_rwkv[:, l, dd]) for dd in range(2)) for l in range(n_layers)]
    x_lat, _, _ = _trunk(
        x_sample, mods, lambda i: i // (ts // tm_s), lambda i: i // (ts // tm_so), tm_s, tm_so, grid_w,
        rwkv_init, [state_lru[:, l] for l in range(n_layers)], layers, final_g, heads, w_b)

    new_rwkv = jnp.stack(
        [jnp.stack([_state_from_chains(s_ctx[l][dd], heads) for dd in range(2)], axis=1)
         for l in range(n_layers)], axis=1)
    new_lru = jnp.stack(h_ctx, axis=1)
    return x_ctx, x_lat, new_rwkv, new_lru
```

```python
import functools
import math

import jax
import jax.numpy as jnp
from jax import lax
from jax.experimental import pallas as pl
from jax.experimental.pallas import tpu as pltpu

F32 = jnp.float32
BF16 = jnp.bfloat16

LANES = 128
SUBLANES = 8
HEAD = 64
CHAIN_BATCH = 8
LORA_COLS = 512
RMS_EPS = 1e-6
GN_EPS = 64e-5
LRU_C = 8.0
KK_EPS = 1e-12
DECAY_SCALE = math.exp(-0.5) * math.log2(math.e)
VMEM_LIMIT = 56 * 1024 * 1024

TM_IN = 1024
TN_IN = 1536
ROWS_NORM = 16
TM_OUT = 256
TT_PREP = 512
ROWS_PREP = 64
CW_PREP = 256
TB_SCAN = 64
ROWS_SCAN = 64
CB_LRU = 256
CH_LRU = 512


def _params(*sem):
    return pltpu.CompilerParams(dimension_semantics=sem, vmem_limit_bytes=VMEM_LIMIT)


def _cost(flops, transcendentals, *arrays):
    nbytes = sum(math.prod(a.shape) * jnp.dtype(a.dtype).itemsize for a in arrays)
    return pl.CostEstimate(flops=int(flops), transcendentals=int(transcendentals), bytes_accessed=int(nbytes))


def _softplus(x):
    return jnp.maximum(x, 0.0) + jnp.log1p(jnp.exp(-jnp.abs(x)))


def _sigmoid(x):
    return 0.5 * jnp.tanh(0.5 * x) + 0.5


def _silu(x):
    return x * _sigmoid(x)


def _ada_kernel(c_ref, w_ref, b_ref, o_ref):
    h = _silu(c_ref[...]).astype(BF16)
    o_ref[0] = jnp.dot(h, w_ref[0].astype(BF16), preferred_element_type=F32) + b_ref[0]


def _ada_mod(cond, ada_w, ada_b):
    n_l, d, n = ada_w.shape
    bc = cond.shape[0]
    tn = math.gcd(n, 512)
    return pl.pallas_call(
        _ada_kernel,
        grid=(n_l, n // tn),
        in_specs=[
            pl.BlockSpec((bc, d), lambda l, j: (0, 0)),
            pl.BlockSpec((1, d, tn), lambda l, j: (l, 0, j)),
            pl.BlockSpec((1, 1, tn), lambda l, j: (l, 0, j)),
        ],
        out_specs=pl.BlockSpec((1, bc, tn), lambda l, j: (l, 0, j)),
        out_shape=jax.ShapeDtypeStruct((n_l, bc, n), F32),
        compiler_params=_params("arbitrary", "arbitrary"),
        name="ada",
    )(cond, ada_w, ada_b.reshape(n_l, 1, n))


def _inproj_kernel(x_ref, mod_ref, g_ref, w_ref, wl_ref, o_ref, ol_ref, xm_ref):
    j = pl.program_id(1)

    @pl.when(j == 0)
    def _():
        gain = g_ref[...]
        scale = 1.0 + mod_ref[0, 1:2, :]
        shift = mod_ref[0, 0:1, :]

        def rows_body(c, carry):
            rows = pl.ds(pl.multiple_of(c * ROWS_NORM, ROWS_NORM), ROWS_NORM)
            x = x_ref[rows, :]
            y = x * lax.rsqrt(jnp.mean(x * x, axis=-1, keepdims=True) + RMS_EPS) * gain
            xm_ref[rows, :] = (y * scale + shift).astype(BF16)
            return carry

        lax.fori_loop(0, x_ref.shape[0] // ROWS_NORM, rows_body, 0, unroll=8)

    o_ref[...] = jnp.dot(xm_ref[...], w_ref[...], preferred_element_type=F32)

    @pl.when(j == pl.num_programs(1) - 1)
    def _():
        ol_ref[...] = jnp.dot(xm_ref[...], wl_ref[...], preferred_element_type=F32)


def _col_tile(n, cap):
    return 256 * max(q for q in range(1, cap // 256 + 1) if (n // 256) % q == 0)


def _inproj(x2, mod, norm_g, w_main, w_lora, tm, mod_row):
    m, d = x2.shape
    n = w_main.shape[1]
    nl = w_lora.shape[1]
    tn = _col_tile(n, TN_IN)
    outs = [jax.ShapeDtypeStruct((m, n), F32), jax.ShapeDtypeStruct((m, nl), F32)]
    return pl.pallas_call(
        _inproj_kernel,
        grid=(m // tm, n // tn),
        in_specs=[
            pl.BlockSpec((tm, d), lambda i, j: (i, 0)),
            pl.BlockSpec((1, 3, d), lambda i, j: (mod_row(i), 0, 0)),
            pl.BlockSpec((1, d), lambda i, j: (0, 0)),
            pl.BlockSpec((d, tn), lambda i, j: (0, j)),
            pl.BlockSpec((d, nl), lambda i, j: (0, 0)),
        ],
        out_specs=[pl.BlockSpec((tm, tn), lambda i, j: (i, j)), pl.BlockSpec((tm, nl), lambda i, j: (i, 0))],
        out_shape=outs,
        scratch_shapes=[pltpu.VMEM((tm, d), BF16)],
        compiler_params=_params("arbitrary", "arbitrary"),
        cost_estimate=_cost(2 * m * d * (n + nl), m, x2, w_main, w_lora, *outs),
        name="inproj",
    )(x2, mod, norm_g.reshape(1, d), w_main, w_lora)


def _prep_kernel(*refs, tt, cw, grid_w, n_t, has_vres):
    it = iter(refs)
    zr, zk, zv = next(it), next(it), next(it)
    halos = [next(it) for _ in range(6)] if grid_w else [None] * 6
    lo_ref = next(it)
    vfirst = next(it) if has_vres else None
    mu, dw0, dw2, iw0, iw2 = (next(it) for _ in range(5))
    vw0, vw2 = (next(it), next(it)) if has_vres else (None, None)
    r_o, k_o, v_o, wf_o, wb_o, af_o, ab_o = (next(it) for _ in range(7))
    td_s, li_s, lv_s, er_s, ek_s, ev_s = (next(it) for _ in range(6))

    @pl.when(pl.program_id(2) == 0)
    def _():
        td_s[...] = jnp.tanh(lo_ref[0, :, 0:256]).astype(BF16)
        li_s[...] = lo_ref[0, :, 128:384].astype(BF16)
        lv_s[...] = lo_ref[0, :, 384:512].astype(BF16)

    pad = grid_w if grid_w else SUBLANES
    for e_s, z_ref, hp, hn in ((er_s, zr, halos[0], halos[1]), (ek_s, zk, halos[2], halos[3]),
                               (ev_s, zv, halos[4], halos[5])):
        if grid_w:
            i = pl.program_id(1)
            e_s[0:pad, :] = jnp.where(i > 0, hp[0], 0.0)
            e_s[pad + tt:2 * pad + tt, :] = jnp.where(i < n_t - 1, hn[0], 0.0)
        else:
            e_s[0:pad, :] = jnp.zeros((pad, cw), F32)
            e_s[pad + tt:2 * pad + tt, :] = jnp.zeros((pad, cw), F32)
        e_s[pad:pad + tt, :] = z_ref[0]

    rc = min(tt, ROWS_PREP)
    col = lax.rem(lax.broadcasted_iota(jnp.int32, (rc, cw), 0), grid_w) if grid_w else None

    def mix(e_s, c0, j):
        z = e_s[pad + c0:pad + c0 + rc, :]
        before = e_s[pad + c0 - 1:pad + c0 - 1 + rc, :]
        after = e_s[pad + c0 + 1:pad + c0 + 1 + rc, :]
        if grid_w:
            up = e_s[c0:c0 + rc, :]
            down = e_s[2 * pad + c0:2 * pad + c0 + rc, :]
            left = jnp.where(col == 0, 0.0, before)
            right = jnp.where(col == grid_w - 1, 0.0, after)
            sh = 0.25 * (up + down + left + right)
        else:
            sh = 0.5 * (before + after)
        return z + (sh - z) * mu[j:j + 1, :]

    for c0 in range(0, tt, rc):
        rows = pl.ds(c0, rc)
        r_o[0, rows, :] = mix(er_s, c0, 0)
        k_o[0, rows, :] = mix(ek_s, c0, 1)
        v = mix(ev_s, c0, 2)
        if has_vres:
            gate = _sigmoid(vw0[...] + jnp.dot(lv_s[rows, :], vw2[...], preferred_element_type=F32))
            v = v + (vfirst[0, rows, :] - v) * gate
        v_o[0, rows, :] = v
        td = td_s[rows, :]
        li = li_s[rows, :]
        for d, (w_o, a_o) in enumerate(((wf_o, af_o), (wb_o, ab_o))):
            xd = dw0[d:d + 1, :] + jnp.dot(td, dw2[d], preferred_element_type=F32)
            w_o[0, rows, :] = -DECAY_SCALE * _sigmoid(xd)
            a_o[0, rows, :] = _sigmoid(iw0[d:d + 1, :] + jnp.dot(li, iw2[d], preferred_element_type=F32))


def _prep(z3, zl3, v_first, lw, grid_w, w_a):
    b, t, _ = z3.shape
    cw = CW_PREP
    nc = w_a // cw
    has_vres = v_first is not None
    if grid_w:
        tt = min(TT_PREP, t)
        assert min(tt, ROWS_PREP) % grid_w == 0 and grid_w % SUBLANES == 0 and t % tt == 0
    else:
        tt = t
    n_t = t // tt
    blk = lambda off: pl.BlockSpec((1, tt, cw), lambda bb, i, c, off=off: (bb, i, off + c))
    in_specs = [blk(0), blk(nc), blk(2 * nc)]
    args = [z3, z3, z3]
    if grid_w:
        per, last = tt // grid_w, t // grid_w - 1
        for off in (0, nc, 2 * nc):
            in_specs.append(pl.BlockSpec(
                (1, grid_w, cw),
                lambda bb, i, c, off=off: (bb, jnp.maximum(i * per - 1, 0), off + c)))
            in_specs.append(pl.BlockSpec(
                (1, grid_w, cw),
                lambda bb, i, c, off=off: (bb, jnp.minimum((i + 1) * per, last), off + c)))
            args += [z3, z3]
    in_specs.append(pl.BlockSpec((1, tt, LORA_COLS), lambda bb, i, c: (bb, i, 0)))
    args.append(zl3)
    if has_vres:
        in_specs.append(blk(0))
        args.append(v_first)
    wcol = lambda shape: pl.BlockSpec(shape, lambda bb, i, c: (0,) * (len(shape) - 1) + (c,))
    in_specs += [wcol((3, cw)), wcol((2, cw)), wcol((2, 256, cw)), wcol((2, cw)), wcol((2, 256, cw))]
    args += [lw["mu"], lw["dec_w0"], lw["dec_w2p"], lw["iclr_w0"], lw["iclr_w2p"]]
    if has_vres:
        in_specs += [wcol((1, cw)), wcol((128, cw))]
        args += [lw["vres_w0"], lw["vres_w2p"]]
    out = jax.ShapeDtypeStruct((b, t, w_a), F32)
    return pl.pallas_call(
        functools.partial(_prep_kernel, tt=tt, cw=cw, grid_w=grid_w, n_t=n_t, has_vres=has_vres),
        grid=(b, n_t, nc),
        in_specs=in_specs,
        out_specs=[blk(0)] * 7,
        out_shape=[out] * 7,
        scratch_shapes=[pltpu.VMEM((tt, 256), BF16), pltpu.VMEM((tt, 256), BF16), pltpu.VMEM((tt, 128), BF16)]
        + [pltpu.VMEM((tt + 2 * (grid_w if grid_w else SUBLANES), cw), F32)] * 3,
        compiler_params=_params("arbitrary", "arbitrary", "arbitrary"),
        cost_estimate=_cost(b * t * w_a * (2 * 5 * 256 + 60), 6 * b * t * w_a, *[out] * 11),
        name="prep",
    )(*args)


def _scan_kernel(*refs, tb, reverse, merge):
    it = iter(refs)
    r_ref, k_ref, v_ref, lw_ref, a_ref, kkc_ref, kac_ref, rkc_ref, s0_ref = (next(it) for _ in range(9))
    yo_ref, boo_ref, g_ref, b_ref = (next(it) for _ in range(4)) if merge else (None,) * 4
    y_ref, bo_ref, sfin_ref, s_scr, al_s, be_s, kt_s, rt_s = (next(it) for _ in range(8))
    assert not merge or ROWS_SCAN == HEAD, "the merged group norm needs a whole head per row group"
    i = pl.program_id(1)

    @pl.when(i == 0)
    def _():
        s_scr[...] = s0_ref[0]

    off = 1 if reverse else 0
    al_s[0 if reverse else tb] = jnp.zeros((HEAD, LANES), F32)

    kac = kac_ref[...]
    one_minus_kac = 1.0 - kac

    def prep_step(s, carry):
        lcum, wprev = carry
        t = (tb - 1 - s) if reverse else s
        k_t, a_t, r_t = k_ref[0, t], a_ref[0, t], r_ref[0, t]
        kkr = k_t * kkc_ref[...]
        nrm = jnp.sqrt(jnp.sum(kkr * kkr, axis=0, keepdims=True))
        kk = kkr / jnp.maximum(nrm, KK_EPS)
        al_s[t + off] = wprev * kk
        lcum = lcum + lw_ref[0, t]
        wcum = jnp.exp2(lcum)
        inv = jnp.exp2(-lcum)
        kd = k_t * (a_t * kac + one_minus_kac)
        be_s[t] = (kk * a_t) * inv
        kt_s[t] = kd * inv
        rt_s[t] = wcum * r_t
        bo = jnp.sum((r_t * kd) * rkc_ref[...], axis=0, keepdims=True)
        bo_ref[0, t] = jnp.broadcast_to(bo, (SUBLANES, LANES))
        return lcum, wcum

    _, w_last = lax.fori_loop(0, tb, prep_step, (jnp.zeros((HEAD, LANES), F32), jnp.ones((HEAD, LANES), F32)),
                              unroll=4)

    groups = [pl.ds(g * ROWS_SCAN, ROWS_SCAN) for g in range(HEAD // ROWS_SCAN)]

    def s_dot_al(rows, slot):
        acc = [None, None]
        for kx in range(HEAD):
            term = s_scr[kx, rows, :] * al_s[slot, pl.ds(kx, 1), :]
            acc[kx % 2] = term if acc[kx % 2] is None else acc[kx % 2] + term
        return acc[0] + acc[1]

    def time_of(s):
        return (tb - 1 - s) if reverse else s

    def finish(s):
        t = time_of(s)
        y = yo_ref[0, t] + y_ref[0, t]
        dlt = y - jnp.mean(y, axis=0, keepdims=True)
        var = jnp.mean(dlt * dlt, axis=0, keepdims=True)
        bo = boo_ref[0, t, 0:1, :] + bo_ref[0, t, 0:1, :]
        y_ref[0, t] = dlt * lax.rsqrt(var + GN_EPS) * g_ref[...] + b_ref[...] + bo * v_ref[0, t]

    def step(s, sal_all):
        t = time_of(s)
        nxt = t if reverse else t + 1
        sal_next = []
        for rows, sal in zip(groups, sal_all):
            vv = v_ref[0, t, rows, :]
            yac = acc = None
            for kx in range(HEAD):
                kr = pl.ds(kx, 1)
                sn = s_scr[kx, rows, :] - sal * be_s[t, kr, :] + vv * kt_s[t, kr, :]
                s_scr[kx, rows, :] = sn
                ty = sn * rt_s[t, kr, :]
                ta = sn * al_s[nxt, kr, :]
                yac = ty if yac is None else yac + ty
                acc = ta if acc is None else acc + ta
            y_ref[0, t, rows, :] = yac
            sal_next.append(acc)
        return tuple(sal_next)

    def step_and_finish(s, sal_all):
        finish(s - 1)
        return step(s, sal_all)

    first = (tb - 1 + off) if reverse else 0
    sal0 = tuple(s_dot_al(rows, first) for rows in groups)
    if merge:
        lax.fori_loop(1, tb, step_and_finish, step(0, sal0))
        finish(tb - 1)
    else:
        lax.fori_loop(0, tb, step, sal0)

    wl_s = be_s.at[0]
    wl_s[...] = w_last
    for kx in range(HEAD):
        s_scr[kx] = s_scr[kx] * wl_s[pl.ds(kx, 1), :]

    @pl.when(i == pl.num_programs(1) - 1)
    def _():
        sfin_ref[0] = s_scr[...]


def _scan(r, k, v, lw, a, kk_c, ka_c, rk_c, s0, reverse, merge=()):
    nt, t = r.shape[0], r.shape[1]
    tb = min(TB_SCAN, t)
    nb = t // tb
    tmap = (lambda c, i: (c, nb - 1 - i, 0, 0)) if reverse else (lambda c, i: (c, i, 0, 0))
    seq = pl.BlockSpec((1, tb, HEAD, LANES), tmap)
    row = pl.BlockSpec((1, tb, SUBLANES, LANES), tmap)
    cst = pl.BlockSpec((HEAD, LANES), lambda c, i: (0, 0))
    st = pl.BlockSpec((1, HEAD, HEAD, LANES), lambda c, i: (c, 0, 0, 0))
    return pl.pallas_call(
        functools.partial(_scan_kernel, tb=tb, reverse=reverse, merge=bool(merge)),
        grid=(nt, nb),
        in_specs=[seq] * 5 + [cst, cst, cst, st] + ([seq, row, cst, cst] if merge else []),
        out_specs=[seq, row, st],
        out_shape=[jax.ShapeDtypeStruct(r.shape, F32), jax.ShapeDtypeStruct((nt, t, SUBLANES, LANES), F32),
                   jax.ShapeDtypeStruct(s0.shape, F32)],
        scratch_shapes=[pltpu.VMEM((HEAD, HEAD, LANES), F32), pltpu.VMEM((tb + 1, HEAD, LANES), F32)]
        + [pltpu.VMEM((tb, HEAD, LANES), F32)] * 3,
        compiler_params=_params("arbitrary", "arbitrary"),
        cost_estimate=_cost(8 * r.size * HEAD, 4 * r.size, r, k, v, lw, a, r, s0, s0, *merge[:1]),
        name="scan_bwd" if reverse else "scan_fwd",
    )(r, k, v, lw, a, kk_c, ka_c, rk_c, s0, *merge)


def _lru_kernel(x_ref, h0_ref, cw_ref, cb_ref, wr_ref, br_ref, wi_ref, bi_ref, lam_ref, o_ref, hfin_ref,
                xpad, af, uf, ab, ub, *, t_len, ch):
    cbk = x_ref.shape[-1]
    pad = SUBLANES
    xpad[0:pad, :] = jnp.zeros((pad, cbk), F32)
    xpad[t_len + pad:t_len + 2 * pad, :] = jnp.zeros((pad, cbk), F32)
    xpad[pad:t_len + pad, :] = x_ref[0]
    cw = cw_ref[...]
    for c0 in range(0, t_len, ch):
        xc = cb_ref[...] + xpad[c0 + pad - 2:c0 + pad - 2 + ch, :] * cw[0:1, :]
        for j in range(1, 4):
            xc = xc + xpad[c0 + pad - 2 + j:c0 + pad - 2 + j + ch, :] * cw[j:j + 1, :]
        xcb = xc.astype(BF16)
        for d, (a_s, u_s) in enumerate(((af, uf), (ab, ub))):
            rg = _sigmoid(jnp.dot(xcb, wr_ref[d, 0], preferred_element_type=F32) + br_ref[d:d + 1, :])
            ig = _sigmoid(jnp.dot(xcb, wi_ref[d, 0], preferred_element_type=F32) + bi_ref[d:d + 1, :])
            log_a = (-LRU_C * _softplus(-lam_ref[d:d + 1, :])) * rg
            a = jnp.exp(log_a)
            a_s[c0:c0 + ch, :] = a
            s2 = 1.0 - a * a
            root = jnp.where(s2 > 0.0, s2 * lax.rsqrt(s2), 0.0)
            u_s[c0:c0 + ch, :] = root * (ig * xc)

    rowi = lax.broadcasted_iota(jnp.int32, (SUBLANES, cbk), 0)

    def tile_scan(a, u, h_in, backward):
        for s in (1, 2, 4):
            sh = SUBLANES - s if backward else s
            a_s, u_s = pltpu.roll(a, sh, 0), pltpu.roll(u, sh, 0)
            m = (rowi < SUBLANES - s) if backward else (rowi >= s)
            u = jnp.where(m, u + a * u_s, u)
            a = jnp.where(m, a * a_s, a)
        h = a * h_in + u
        return h, (h[0:1, :] if backward else h[SUBLANES - 1:SUBLANES, :])

    def body(i, carry):
        hf, hb = carry
        of = pl.multiple_of(i * SUBLANES, SUBLANES)
        h, hf = tile_scan(af[pl.ds(of, SUBLANES), :], uf[pl.ds(of, SUBLANES), :], hf, False)
        af[pl.ds(of, SUBLANES), :] = h
        ob = pl.multiple_of(t_len - SUBLANES - i * SUBLANES, SUBLANES)
        h, hb = tile_scan(ab[pl.ds(ob, SUBLANES), :], ub[pl.ds(ob, SUBLANES), :], hb, True)
        ab[pl.ds(ob, SUBLANES), :] = h
        return hf, hb

    hf, hb = lax.fori_loop(0, t_len // SUBLANES, body, (h0_ref[0, 0:1, :], h0_ref[0, 1:2, :]), unroll=8)
    o_ref[0] = af[...] + ab[...]
    hfin_ref[0, 0:1, :] = hf
    hfin_ref[0, 1:2, :] = hb


def _lru(z3, h0, lw, w_a, w_b):
    b, t, _ = z3.shape
    cbk = CB_LRU
    ncb = w_b // cbk
    off = 4 * w_a // cbk
    ch = min(CH_LRU, t)
    wcol = lambda rows: pl.BlockSpec((rows, cbk), lambda bb, c: (0, c))
    wgate = pl.BlockSpec((2, 1, cbk, cbk), lambda bb, c: (0, c, 0, 0))
    st = pl.BlockSpec((1, 2, cbk), lambda bb, c: (bb, 0, c))
    return pl.pallas_call(
        functools.partial(_lru_kernel, t_len=t, ch=ch),
        grid=(b, ncb),
        in_specs=[pl.BlockSpec((1, t, cbk), lambda bb, c: (bb, 0, off + c)), st,
                  wcol(4), wcol(1), wgate, wcol(2), wgate, wcol(2), wcol(2)],
        out_specs=[pl.BlockSpec((1, t, cbk), lambda bb, c: (bb, 0, c)), st],
        out_shape=[jax.ShapeDtypeStruct((b, t, w_b), F32), jax.ShapeDtypeStruct((b, 2, w_b), F32)],
        scratch_shapes=[pltpu.VMEM((t + 2 * SUBLANES, cbk), F32)] + [pltpu.VMEM((t, cbk), F32)] * 4,
        compiler_params=_params("arbitrary", "arbitrary"),
        cost_estimate=_cost(b * t * w_b * (8 * cbk + 150), 8 * b * t * w_b,
                            *[jax.ShapeDtypeStruct((b, t, w_b), F32)] * 2),
        name="lru",
    )(z3, h0, lw["conv_w"], lw["conv_b"], lw["gr_bd"], lw["gr_b"], lw["gi_bd"], lw["gi_b"], lw["lam"])


def _out_kernel(y_ref, ga_ref, h_ref, gb_ref, ma_ref, mb_ref, x_ref, mod_ref, woa_ref, wob_ref, wo_ref,
                fg_ref, o_ref, *, final):
    ya = jnp.dot((y_ref[...] * _silu(ga_ref[...])).astype(BF16), woa_ref[...], preferred_element_type=F32)
    yb = jnp.dot((h_ref[...] * _silu(gb_ref[...])).astype(BF16), wob_ref[...], preferred_element_type=F32)
    merged = _sigmoid(ma_ref[...]) * ya + _sigmoid(mb_ref[...]) * yb
    out = jnp.dot(merged.astype(BF16), wo_ref[...], preferred_element_type=F32)
    xn = x_ref[...] + mod_ref[0, 2:3, :] * out
    if final:
        xn = xn * lax.rsqrt(jnp.mean(xn * xn, axis=-1, keepdims=True) + RMS_EPS) * fg_ref[...]
    o_ref[...] = xn


def _out(y2, z2, h2, x2, mod, lw, final_g, tm, mod_row, final, w_a, w_b):
    m, d = x2.shape
    once = dict(pipeline_mode=pl.Buffered(1))
    zoff_ga = 3 * w_a // w_a
    zoff_gb = (4 * w_a + w_b) // w_b
    zoff_ma = (4 * w_a + 2 * w_b) // d
    return pl.pallas_call(
        functools.partial(_out_kernel, final=final),
        grid=(m // tm,),
        in_specs=[
            pl.BlockSpec((tm, w_a), lambda i: (i, 0)),
            pl.BlockSpec((tm, w_a), lambda i: (i, zoff_ga)),
            pl.BlockSpec((tm, w_b), lambda i: (i, 0)),
            pl.BlockSpec((tm, w_b), lambda i: (i, zoff_gb)),
            pl.BlockSpec((tm, d), lambda i: (i, zoff_ma)),
            pl.BlockSpec((tm, d), lambda i: (i, zoff_ma + 1)),
            pl.BlockSpec((tm, d), lambda i: (i, 0)),
            pl.BlockSpec((1, 3, d), lambda i: (mod_row(i), 0, 0)),
            pl.BlockSpec((w_a, d), lambda i: (0, 0), **once),
            pl.BlockSpec((w_b, d), lambda i: (0, 0), **once),
            pl.BlockSpec((d, d), lambda i: (0, 0), **once),
            pl.BlockSpec((1, d), lambda i: (0, 0)),
        ],
        out_specs=pl.BlockSpec((tm, d), lambda i: (i, 0)),
        out_shape=jax.ShapeDtypeStruct((m, d), F32),
        compiler_params=_params("arbitrary"),
        cost_estimate=_cost(2 * m * d * (w_a + w_b + d), 4 * m * d, y2, y2, h2, h2, x2, x2, x2, x2,
                            lw["w_out_a"], lw["w_out_b"], lw["w_out"]),
        name="out_final" if final else "out",
    )(y2, z2, h2, z2, z2, z2, x2, mod, lw["w_out_a"], lw["w_out_b"], lw["w_out"], final_g.reshape(1, d))


def _to_chains(x, heads):
    b, t, _ = x.shape
    nt = b // CHAIN_BATCH
    y = jnp.swapaxes(x.reshape(nt, CHAIN_BATCH, t, heads * HEAD), 1, 2)
    return jnp.swapaxes(y.reshape(nt, t, CHAIN_BATCH * heads, HEAD), 2, 3)


def _from_chains(x, heads):
    nt, t = x.shape[0], x.shape[1]
    y = jnp.swapaxes(x, 2, 3).reshape(nt, t, CHAIN_BATCH, heads * HEAD)
    return jnp.swapaxes(y, 1, 2).reshape(nt * CHAIN_BATCH, t, heads * HEAD)


def _chain_const(p, heads):
    t = p.reshape(heads, HEAD).T
    return jnp.broadcast_to(t[:, None, :], (HEAD, CHAIN_BATCH, heads)).reshape(HEAD, CHAIN_BATCH * heads)


def _state_to_chains(s):
    b, h = s.shape[0], s.shape[1]
    nt = b // CHAIN_BATCH
    s = s.reshape(nt, CHAIN_BATCH, h, HEAD, HEAD)
    return s.transpose(0, 4, 3, 1, 2).reshape(nt, HEAD, HEAD, CHAIN_BATCH * h)


def _state_from_chains(s, heads):
    nt = s.shape[0]
    s = s.reshape(nt, HEAD, HEAD, CHAIN_BATCH, heads)
    return s.transpose(0, 3, 4, 2, 1).reshape(nt * CHAIN_BATCH, heads, HEAD, HEAD)


def _block_diag(w, per):
    h, n, _ = w.shape
    g = h // per
    eye = jnp.eye(per, dtype=w.dtype)
    wg = w.reshape(g, per, n, n)
    return (wg[:, :, :, None, :] * eye[None, :, None, :, None]).reshape(g, per * n, per * n)


def _layer_weights(l, n_layers, heads, w_in, mu_rkv, dec_w0, dec_w1, dec_w2, iclr_w0, iclr_w1, iclr_w2,
                   vres_w0, vres_w1, vres_w2, k_k, k_a, r_k, lnx_g, lnx_b, w_out_a, conv_w, conv_b, gr_w,
                   gr_b, gi_w, gi_b, lru_lambda, w_out_b, w_out):
    d = w_in.shape[1]
    w_a = heads * HEAD
    r_dec, r_iclr, r_vres = dec_w1.shape[-1], iclr_w1.shape[-1], vres_w1.shape[-1]
    assert (r_dec, r_iclr, r_vres) == (96, 96, 64), "LoRA column layout assumes ranks 96/96/64"
    has_vres = l > 0
    vw1 = vres_w1[l - 1] if has_vres else jnp.zeros((d, r_vres), F32)
    used = 2 * r_dec + 2 * r_iclr + r_vres
    w_lora = jnp.concatenate(
        [dec_w1[l, 0], dec_w1[l, 1], iclr_w1[l, 0], iclr_w1[l, 1], vw1,
         jnp.zeros((d, LORA_COLS - used), F32)], axis=1).astype(BF16)
    dec_w2p = jnp.zeros((2, 256, w_a), F32)
    iclr_w2p = jnp.zeros((2, 256, w_a), F32)
    for dd in range(2):
        dec_w2p = dec_w2p.at[dd, r_dec * dd:r_dec * (dd + 1)].set(dec_w2[l, dd])
        o = 2 * r_dec - 128 + r_iclr * dd
        iclr_w2p = iclr_w2p.at[dd, o:o + r_iclr].set(iclr_w2[l, dd])
    lw = dict(
        w_main=w_in[l].astype(BF16), w_lora=w_lora, mu=mu_rkv[l], dec_w0=dec_w0[l], dec_w2p=dec_w2p.astype(BF16), iclr_w0=iclr_w0[l],
        iclr_w2p=iclr_w2p.astype(BF16),
        kk_c=_chain_const(k_k[l], heads), ka_c=_chain_const(k_a[l], heads),
        rk_c=_chain_const(r_k[l].reshape(-1), heads), g_c=_chain_const(lnx_g[l], heads),
        b_c=_chain_const(lnx_b[l], heads),
        conv_w=conv_w[l], conv_b=conv_b[l].reshape(1, -1),
        gr_bd=jnp.stack([_block_diag(gr_w[l, dd], CB_LRU // gr_w.shape[-1]) for dd in range(2)]).astype(BF16),
        gi_bd=jnp.stack([_block_diag(gi_w[l, dd], CB_LRU // gi_w.shape[-1]) for dd in range(2)]).astype(BF16),
        gr_b=gr_b[l], gi_b=gi_b[l], lam=lru_lambda[l],
        w_out_a=w_out_a[l].astype(BF16), w_out_b=w_out_b[l].astype(BF16), w_out=w_out[l].astype(BF16),
    )
    if has_vres:
        vres_w2p = jnp.zeros((128, w_a), F32).at[0:r_vres].set(vres_w2[l - 1])
        lw.update(vres_w0=vres_w0[l - 1].reshape(1, -1), vres_w2p=vres_w2p.astype(BF16))
    return lw


def _trunk(x, mods, mod_row_in, mod_row_out, tm_in, tm_out, grid_w, rwkv_init, lru_init, layers, final_g,
           heads, w_b):
    b, t, d = x.shape
    w_a = heads * HEAD
    n_layers = len(layers)
    x2 = x.reshape(b * t, d)
    v_first = None
    s_out, h_out = [], []
    for l, lw in enumerate(layers):
        z2, zl = _inproj(x2, mods[l], lw["norm_g"], lw["w_main"], lw["w_lora"], tm_in, mod_row_in)
        z3 = z2.reshape(b, t, z2.shape[1])
        r, k, v, lw_f, lw_bk, a_f, a_bk = _prep(z3, zl.reshape(b, t, LORA_COLS), v_first, lw, grid_w, w_a)
        if v_first is None:
            v_first = v
        h_sum, h_fin = _lru(z3, lru_init[l], lw, w_a, w_b)
        rc, kc, vc = (_to_chains(q, heads) for q in (r, k, v))
        consts = (lw["kk_c"], lw["ka_c"], lw["rk_c"])
        y_f, bo_f, s_f = _scan(rc, kc, vc, _to_chains(lw_f, heads), _to_chains(a_f, heads), *consts,
                               rwkv_init[l][0], False)
        y_mix, _, s_b = _scan(rc, kc, vc, _to_chains(lw_bk, heads), _to_chains(a_bk, heads), *consts,
                              rwkv_init[l][1], True, merge=(y_f, bo_f, lw["g_c"], lw["b_c"]))
        y_rwkv = _from_chains(y_mix, heads)
        x2 = _out(y_rwkv.reshape(b * t, w_a), z2, h_sum.reshape(b * t, w_b), x2, mods[l], lw, final_g,
                  tm_out, mod_row_out, l == n_layers - 1, w_a, w_b)
        s_out.append((s_f, s_b))
        h_out.append(h_fin)
    return x2.reshape(b, t, d), s_out, h_out


def kernel(x_prompt, x_sample, c, state_rwkv, state_lru, c_ctx, norm_g, ada_w, ada_b, w_in, mu_rkv, dec_w0,
           dec_w1, dec_w2, iclr_w0, iclr_w1, iclr_w2, vres_w0, vres_w1, vres_w2, k_k, k_a, r_k, lnx_g, lnx_b,
           w_out_a, conv_w, conv_b, gr_w, gr_b, gi_w, gi_b, lru_lambda, w_out_b, w_out, final_g):
    n_layers, d = norm_g.shape
    bp, tp, _ = x_prompt.shape
    bs, ts, _ = x_sample.shape
    heads = r_k.shape[1]
    w_b = conv_w.shape[-1]
    grid_w = int(round(ts ** 0.5))
    assert grid_w * grid_w == ts and heads * CHAIN_BATCH == LANES
    assert bp % CHAIN_BATCH == 0 and bs % CHAIN_BATCH == 0

    layers = []
    for l in range(n_layers):
        lw = _layer_weights(l, n_layers, heads, w_in, mu_rkv, dec_w0, dec_w1, dec_w2, iclr_w0, iclr_w1,
                            iclr_w2, vres_w0, vres_w1, vres_w2, k_k, k_a, r_k, lnx_g, lnx_b, w_out_a, conv_w,
                            conv_b, gr_w, gr_b, gi_w, gi_b, lru_lambda, w_out_b, w_out)
        lw["norm_g"] = norm_g[l]
        layers.append(lw)

    ctx_row = bs
    n_rows = -(-(bs + 1) // SUBLANES) * SUBLANES
    cond = jnp.zeros((n_rows, d), F32).at[0:bs].set(c).at[ctx_row].set(c_ctx)
    mod = _ada_mod(cond, ada_w, ada_b).reshape(n_layers, n_rows, 3, d)
    mods = [mod[l] for l in range(n_layers)]

    zero_s = jnp.zeros((bp // CHAIN_BATCH, HEAD, HEAD, LANES), F32)
    tm_p = min(TM_IN, bp * tp)
    tm_po = min(TM_OUT, bp * tp)
    x_ctx, s_ctx, h_ctx = _trunk(
        x_prompt, mods, lambda i: ctx_row, lambda i: ctx_row, tm_p, tm_po, 0,
        [(zero_s, zero_s)] * n_layers, [jnp.zeros((bp, 2, w_b), F32)] * n_layers, layers, final_g, heads, w_b)

    tm_s = min(TM_IN, ts)
    tm_so = min(TM_OUT, ts)
    assert ts % tm_s == 0 and ts % tm_so == 0
    rwkv_init = [tuple(_state_to_chains(state_rwkv[:, l, dd]) for dd in range(2)) for l in range(n_layers)]
    x_lat, _, _ = _trunk(
        x_sample, mods, lambda i: i // (ts // tm_s), lambda i: i // (ts // tm_so), tm_s, tm_so, grid_w,
        rwkv_init, [state_lru[:, l] for l in range(n_layers)], layers, final_g, heads, w_b)

    new_rwkv = jnp.stack(
        [jnp.stack([_state_from_chains(s_ctx[l][dd], heads) for dd in range(2)], axis=1)
         for l in range(n_layers)], axis=1)
    new_lru = jnp.stack(h_ctx, axis=1)
    return x_ctx, x_lat, new_rwkv, new_lru
```

```python
import functools
import math

import jax
import jax.numpy as jnp
from jax import lax
from jax.experimental import pallas as pl
from jax.experimental.pallas import tpu as pltpu

F32 = jnp.float32
BF16 = jnp.bfloat16

LANES = 128
SUBLANES = 8
HEAD = 64
CHAIN_BATCH = 8
LORA_COLS = 512
RMS_EPS = 1e-6
GN_EPS = 64e-5
LRU_C = 8.0
KK_EPS = 1e-12
DECAY_SCALE = math.exp(-0.5) * math.log2(math.e)
VMEM_LIMIT = 56 * 1024 * 1024

TM_IN = 1024
TN_IN = 1536
ROWS_NORM = 16
TM_OUT = 256
TT_PREP = 512
ROWS_PREP = 64
CW_PREP = 256
TB_SCAN = 64
ROWS_SCAN = 64
CB_LRU = 256
CH_LRU = 512


def _params(*sem):
    return pltpu.CompilerParams(dimension_semantics=sem, vmem_limit_bytes=VMEM_LIMIT)


def _cost(flops, transcendentals, *arrays):
    nbytes = sum(math.prod(a.shape) * jnp.dtype(a.dtype).itemsize for a in arrays)
    return pl.CostEstimate(flops=int(flops), transcendentals=int(transcendentals), bytes_accessed=int(nbytes))


def _softplus(x):
    return jnp.maximum(x, 0.0) + jnp.log1p(jnp.exp(-jnp.abs(x)))


def _sigmoid(x):
    return 0.5 * jnp.tanh(0.5 * x) + 0.5


def _silu(x):
    return x * _sigmoid(x)


def _ada_kernel(c_ref, w_ref, b_ref, o_ref):
    h = _silu(c_ref[...]).astype(BF16)
    o_ref[0] = jnp.dot(h, w_ref[0].astype(BF16), preferred_element_type=F32) + b_ref[0]


def _ada_mod(cond, ada_w, ada_b):
    n_l, d, n = ada_w.shape
    bc = cond.shape[0]
    tn = math.gcd(n, 512)
    return pl.pallas_call(
        _ada_kernel,
        grid=(n_l, n // tn),
        in_specs=[
            pl.BlockSpec((bc, d), lambda l, j: (0, 0)),
            pl.BlockSpec((1, d, tn), lambda l, j: (l, 0, j)),
            pl.BlockSpec((1, 1, tn), lambda l, j: (l, 0, j)),
        ],
        out_specs=pl.BlockSpec((1, bc, tn), lambda l, j: (l, 0, j)),
        out_shape=jax.ShapeDtypeStruct((n_l, bc, n), F32),
        compiler_params=_params("arbitrary", "arbitrary"),
        name="ada",
    )(cond, ada_w, ada_b.reshape(n_l, 1, n))


def _inproj_kernel(x_ref, mod_ref, g_ref, w_ref, wl_ref, o_ref, ol_ref, xm_ref):
    j = pl.program_id(1)

    @pl.when(j == 0)
    def _():
        gain = g_ref[...]
        scale = 1.0 + mod_ref[0, 1:2, :]
        shift = mod_ref[0, 0:1, :]

        def rows_body(c, carry):
            rows = pl.ds(pl.multiple_of(c * ROWS_NORM, ROWS_NORM), ROWS_NORM)
            x = x_ref[rows, :]
            y = x * lax.rsqrt(jnp.mean(x * x, axis=-1, keepdims=True) + RMS_EPS) * gain
            xm_ref[rows, :] = (y * scale + shift).astype(BF16)
            return carry

        lax.fori_loop(0, x_ref.shape[0] // ROWS_NORM, rows_body, 0, unroll=8)

    o_ref[...] = jnp.dot(xm_ref[...], w_ref[...], preferred_element_type=F32)

    @pl.when(j == pl.num_programs(1) - 1)
    def _():
        ol_ref[...] = jnp.dot(xm_ref[...], wl_ref[...], preferred_element_type=F32)


def _col_tile(n, cap):
    return 256 * max(q for q in range(1, cap // 256 + 1) if (n // 256) % q == 0)


def _inproj(x2, mod, norm_g, w_main, layer, w_lora, tm, mod_row):
    m, d = x2.shape
    n = w_main.shape[2]
    nl = w_lora.shape[1]
    tn = _col_tile(n, TN_IN)
    outs = [jax.ShapeDtypeStruct((m, n), F32), jax.ShapeDtypeStruct((m, nl), F32)]
    return pl.pallas_call(
        _inproj_kernel,
        grid=(m // tm, n // tn),
        in_specs=[
            pl.BlockSpec((tm, d), lambda i, j: (i, 0)),
            pl.BlockSpec((1, 3, d), lambda i, j: (mod_row(i), 0, 0)),
            pl.BlockSpec((1, d), lambda i, j: (0, 0)),
            pl.BlockSpec((None, d, tn), lambda i, j: (layer, 0, j)),
            pl.BlockSpec((d, nl), lambda i, j: (0, 0)),
        ],
        out_specs=[pl.BlockSpec((tm, tn), lambda i, j: (i, j)), pl.BlockSpec((tm, nl), lambda i, j: (i, 0))],
        out_shape=outs,
        scratch_shapes=[pltpu.VMEM((tm, d), BF16)],
        compiler_params=_params("arbitrary", "arbitrary"),
        cost_estimate=_cost(2 * m * d * (n + nl), m, x2, jax.ShapeDtypeStruct((d, n), BF16), w_lora, *outs),
        name="inproj",
    )(x2, mod, norm_g.reshape(1, d), w_main, w_lora)


def _prep_kernel(*refs, tt, cw, grid_w, n_t, has_vres):
    it = iter(refs)
    zr, zk, zv = next(it), next(it), next(it)
    halos = [next(it) for _ in range(6)] if grid_w else [None] * 6
    lo_ref = next(it)
    vfirst = next(it) if has_vres else None
    mu, dw0, dw2, iw0, iw2 = (next(it) for _ in range(5))
    vw0, vw2 = (next(it), next(it)) if has_vres else (None, None)
    r_o, k_o, v_o, wf_o, wb_o, af_o, ab_o = (next(it) for _ in range(7))
    td_s, li_s, lv_s, er_s, ek_s, ev_s = (next(it) for _ in range(6))

    @pl.when(pl.program_id(2) == 0)
    def _():
        td_s[...] = jnp.tanh(lo_ref[0, :, 0:256]).astype(BF16)
        li_s[...] = lo_ref[0, :, 128:384].astype(BF16)
        lv_s[...] = lo_ref[0, :, 384:512].astype(BF16)

    pad = grid_w if grid_w else SUBLANES
    for e_s, z_ref, hp, hn in ((er_s, zr, halos[0], halos[1]), (ek_s, zk, halos[2], halos[3]),
                               (ev_s, zv, halos[4], halos[5])):
        if grid_w:
            i = pl.program_id(1)
            e_s[0:pad, :] = jnp.where(i > 0, hp[0], 0.0)
            e_s[pad + tt:2 * pad + tt, :] = jnp.where(i < n_t - 1, hn[0], 0.0)
        else:
            e_s[0:pad, :] = jnp.zeros((pad, cw), F32)
            e_s[pad + tt:2 * pad + tt, :] = jnp.zeros((pad, cw), F32)
        e_s[pad:pad + tt, :] = z_ref[0]

    rc = min(tt, ROWS_PREP)
    col = lax.rem(lax.broadcasted_iota(jnp.int32, (rc, cw), 0), grid_w) if grid_w else None

    def mix(e_s, c0, j):
        z = e_s[pad + c0:pad + c0 + rc, :]
        before = e_s[pad + c0 - 1:pad + c0 - 1 + rc, :]
        after = e_s[pad + c0 + 1:pad + c0 + 1 + rc, :]
        if grid_w:
            up = e_s[c0:c0 + rc, :]
            down = e_s[2 * pad + c0:2 * pad + c0 + rc, :]
            left = jnp.where(col == 0, 0.0, before)
            right = jnp.where(col == grid_w - 1, 0.0, after)
            sh = 0.25 * (up + down + left + right)
        else:
            sh = 0.5 * (before + after)
        return z + (sh - z) * mu[j:j + 1, :]

    for c0 in range(0, tt, rc):
        rows = pl.ds(c0, rc)
        r_o[0, rows, :] = mix(er_s, c0, 0)
        k_o[0, rows, :] = mix(ek_s, c0, 1)
        v = mix(ev_s, c0, 2)
        if has_vres:
            gate = _sigmoid(vw0[...] + jnp.dot(lv_s[rows, :], vw2[...], preferred_element_type=F32))
            v = v + (vfirst[0, rows, :] - v) * gate
        v_o[0, rows, :] = v
        td = td_s[rows, :]
        li = li_s[rows, :]
        for d, (w_o, a_o) in enumerate(((wf_o, af_o), (wb_o, ab_o))):
            xd = dw0[d:d + 1, :] + jnp.dot(td, dw2[d], preferred_element_type=F32)
            w_o[0, rows, :] = -DECAY_SCALE * _sigmoid(xd)
            a_o[0, rows, :] = _sigmoid(iw0[d:d + 1, :] + jnp.dot(li, iw2[d], preferred_element_type=F32))


def _prep(z3, zl3, v_first, lw, grid_w, w_a):
    b, t, _ = z3.shape
    cw = CW_PREP
    nc = w_a // cw
    has_vres = v_first is not None
    if grid_w:
        tt = min(TT_PREP, t)
        assert min(tt, ROWS_PREP) % grid_w == 0 and grid_w % SUBLANES == 0 and t % tt == 0
    else:
        tt = t
    n_t = t // tt
    blk = lambda off: pl.BlockSpec((1, tt, cw), lambda bb, i, c, off=off: (bb, i, off + c))
    in_specs = [blk(0), blk(nc), blk(2 * nc)]
    args = [z3, z3, z3]
    if grid_w:
        per, last = tt // grid_w, t // grid_w - 1
        for off in (0, nc, 2 * nc):
            in_specs.append(pl.BlockSpec(
                (1, grid_w, cw),
                lambda bb, i, c, off=off: (bb, jnp.maximum(i * per - 1, 0), off + c)))
            in_specs.append(pl.BlockSpec(
                (1, grid_w, cw),
                lambda bb, i, c, off=off: (bb, jnp.minimum((i + 1) * per, last), off + c)))
            args += [z3, z3]
    in_specs.append(pl.BlockSpec((1, tt, LORA_COLS), lambda bb, i, c: (bb, i, 0)))
    args.append(zl3)
    if has_vres:
        in_specs.append(blk(0))
        args.append(v_first)
    wcol = lambda shape: pl.BlockSpec(shape, lambda bb, i, c: (0,) * (len(shape) - 1) + (c,))
    in_specs += [wcol((3, cw)), wcol((2, cw)), wcol((2, 256, cw)), wcol((2, cw)), wcol((2, 256, cw))]
    args += [lw["mu"], lw["dec_w0"], lw["dec_w2p"], lw["iclr_w0"], lw["iclr_w2p"]]
    if has_vres:
        in_specs += [wcol((1, cw)), wcol((128, cw))]
        args += [lw["vres_w0"], lw["vres_w2p"]]
    out = jax.ShapeDtypeStruct((b, t, w_a), F32)
    return pl.pallas_call(
        functools.partial(_prep_kernel, tt=tt, cw=cw, grid_w=grid_w, n_t=n_t, has_vres=has_vres),
        grid=(b, n_t, nc),
        in_specs=in_specs,
        out_specs=[blk(0)] * 7,
        out_shape=[out] * 7,
        scratch_shapes=[pltpu.VMEM((tt, 256), BF16), pltpu.VMEM((tt, 256), BF16), pltpu.VMEM((tt, 128), BF16)]
        + [pltpu.VMEM((tt + 2 * (grid_w if grid_w else SUBLANES), cw), F32)] * 3,
        compiler_params=_params("arbitrary", "arbitrary", "arbitrary"),
        cost_estimate=_cost(b * t * w_a * (2 * 5 * 256 + 60), 6 * b * t * w_a, *[out] * 11),
        name="prep",
    )(*args)


def _scan_kernel(*refs, tb, reverse, merge):
    it = iter(refs)
    r_ref, k_ref, v_ref, lw_ref, a_ref, kkc_ref, kac_ref, rkc_ref, s0_ref = (next(it) for _ in range(9))
    yo_ref, boo_ref, g_ref, b_ref = (next(it) for _ in range(4)) if merge else (None,) * 4
    y_ref, bo_ref, sfin_ref, s_scr, al_s, be_s, kt_s, rt_s = (next(it) for _ in range(8))
    assert not merge or ROWS_SCAN == HEAD, "the merged group norm needs a whole head per row group"
    i = pl.program_id(1)

    @pl.when(i == 0)
    def _():
        s_scr[...] = s0_ref[0]

    off = 1 if reverse else 0
    al_s[0 if reverse else tb] = jnp.zeros((HEAD, LANES), F32)

    kac = kac_ref[...]
    one_minus_kac = 1.0 - kac

    def prep_step(s, carry):
        lcum, wprev = carry
        t = (tb - 1 - s) if reverse else s
        k_t, a_t, r_t = k_ref[0, t], a_ref[0, t], r_ref[0, t]
        kkr = k_t * kkc_ref[...]
        nrm = jnp.sqrt(jnp.sum(kkr * kkr, axis=0, keepdims=True))
        kk = kkr * (1.0 / jnp.maximum(nrm, KK_EPS))
        al_s[t + off] = wprev * kk
        lcum = lcum + lw_ref[0, t]
        wcum = jnp.exp2(lcum)
        inv = jnp.exp2(-lcum)
        kd = k_t * (a_t * kac + one_minus_kac)
        be_s[t] = (kk * a_t) * inv
        kt_s[t] = kd * inv
        rt_s[t] = wcum * r_t
        bo = jnp.sum((r_t * kd) * rkc_ref[...], axis=0, keepdims=True)
        bo_ref[0, t] = jnp.broadcast_to(bo, (SUBLANES, LANES))
        return lcum, wcum

    _, w_last = lax.fori_loop(0, tb, prep_step, (jnp.zeros((HEAD, LANES), F32), jnp.ones((HEAD, LANES), F32)),
                              unroll=4)

    groups = [pl.ds(g * ROWS_SCAN, ROWS_SCAN) for g in range(HEAD // ROWS_SCAN)]

    def s_dot_al(rows, slot):
        acc = [None, None]
        for kx in range(HEAD):
            term = s_scr[kx, rows, :] * al_s[slot, pl.ds(kx, 1), :]
            acc[kx % 2] = term if acc[kx % 2] is None else acc[kx % 2] + term
        return acc[0] + acc[1]

    def time_of(s):
        return (tb - 1 - s) if reverse else s

    def finish(s):
        t = time_of(s)
        y = yo_ref[0, t] + y_ref[0, t]
        dlt = y - jnp.mean(y, axis=0, keepdims=True)
        var = jnp.mean(dlt * dlt, axis=0, keepdims=True)
        bo = boo_ref[0, t, 0:1, :] + bo_ref[0, t, 0:1, :]
        y_ref[0, t] = dlt * lax.rsqrt(var + GN_EPS) * g_ref[...] + b_ref[...] + bo * v_ref[0, t]

    def step(s, sal_all):
        t = time_of(s)
        nxt = t if reverse else t + 1
        sal_next = []
        for rows, sal in zip(groups, sal_all):
            vv = v_ref[0, t, rows, :]
            yac = acc = None
            for kx in range(HEAD):
                kr = pl.ds(kx, 1)
                sn = s_scr[kx, rows, :] - sal * be_s[t, kr, :] + vv * kt_s[t, kr, :]
                s_scr[kx, rows, :] = sn
                ty = sn * rt_s[t, kr, :]
                ta = sn * al_s[nxt, kr, :]
                yac = ty if yac is None else yac + ty
                acc = ta if acc is None else acc + ta
            y_ref[0, t, rows, :] = yac
            sal_next.append(acc)
        return tuple(sal_next)

    def step_and_finish(s, sal_all):
        finish(s - 1)
        return step(s, sal_all)

    first = (tb - 1 + off) if reverse else 0
    sal0 = tuple(s_dot_al(rows, first) for rows in groups)
    if merge:
        lax.fori_loop(1, tb, step_and_finish, step(0, sal0))
        finish(tb - 1)
    else:
        lax.fori_loop(0, tb, step, sal0)

    wl_s = be_s.at[0]
    wl_s[...] = w_last
    for kx in range(HEAD):
        s_scr[kx] = s_scr[kx] * wl_s[pl.ds(kx, 1), :]

    @pl.when(i == pl.num_programs(1) - 1)
    def _():
        sfin_ref[0] = s_scr[...]


def _scan(r, k, v, lw, a, kk_c, ka_c, rk_c, s0, reverse, merge=()):
    nt, t = r.shape[0], r.shape[1]
    tb = min(TB_SCAN, t)
    nb = t // tb
    tmap = (lambda c, i: (c, nb - 1 - i, 0, 0)) if reverse else (lambda c, i: (c, i, 0, 0))
    seq = pl.BlockSpec((1, tb, HEAD, LANES), tmap)
    row = pl.BlockSpec((1, tb, SUBLANES, LANES), tmap)
    cst = pl.BlockSpec((HEAD, LANES), lambda c, i: (0, 0))
    st = pl.BlockSpec((1, HEAD, HEAD, LANES), lambda c, i: (c, 0, 0, 0))
    return pl.pallas_call(
        functools.partial(_scan_kernel, tb=tb, reverse=reverse, merge=bool(merge)),
        grid=(nt, nb),
        in_specs=[seq] * 5 + [cst, cst, cst, st] + ([seq, row, cst, cst] if merge else []),
        out_specs=[seq, row, st],
        out_shape=[jax.ShapeDtypeStruct(r.shape, F32), jax.ShapeDtypeStruct((nt, t, SUBLANES, LANES), F32),
                   jax.ShapeDtypeStruct(s0.shape, F32)],
        scratch_shapes=[pltpu.VMEM((HEAD, HEAD, LANES), F32), pltpu.VMEM((tb + 1, HEAD, LANES), F32)]
        + [pltpu.VMEM((tb, HEAD, LANES), F32)] * 3,
        compiler_params=_params("arbitrary", "arbitrary"),
        cost_estimate=_cost(8 * r.size * HEAD, 4 * r.size, r, k, v, lw, a, r, s0, s0, *merge[:1]),
        name="scan_bwd" if reverse else "scan_fwd",
    )(r, k, v, lw, a, kk_c, ka_c, rk_c, s0, *merge)


def _lru_kernel(x_ref, h0_ref, cw_ref, cb_ref, wr_ref, br_ref, wi_ref, bi_ref, lam_ref, o_ref, hfin_ref,
                xpad, af, uf, ab, ub, *, t_len, ch):
    cbk = x_ref.shape[-1]
    pad = SUBLANES
    xpad[0:pad, :] = jnp.zeros((pad, cbk), F32)
    xpad[t_len + pad:t_len + 2 * pad, :] = jnp.zeros((pad, cbk), F32)
    xpad[pad:t_len + pad, :] = x_ref[0]
    cw = cw_ref[...]
    for c0 in range(0, t_len, ch):
        xc = cb_ref[...] + xpad[c0 + pad - 2:c0 + pad - 2 + ch, :] * cw[0:1, :]
        for j in range(1, 4):
            xc = xc + xpad[c0 + pad - 2 + j:c0 + pad - 2 + j + ch, :] * cw[j:j + 1, :]
        xcb = xc.astype(BF16)
        for d, (a_s, u_s) in enumerate(((af, uf), (ab, ub))):
            rg = _sigmoid(jnp.dot(xcb, wr_ref[d, 0], preferred_element_type=F32) + br_ref[d:d + 1, :])
            ig = _sigmoid(jnp.dot(xcb, wi_ref[d, 0], preferred_element_type=F32) + bi_ref[d:d + 1, :])
            log_a = (-LRU_C * _softplus(-lam_ref[d:d + 1, :])) * rg
            a = jnp.exp(log_a)
            a_s[c0:c0 + ch, :] = a
            s2 = 1.0 - a * a
            root = jnp.where(s2 > 0.0, s2 * lax.rsqrt(s2), 0.0)
            u_s[c0:c0 + ch, :] = root * (ig * xc)

    rowi = lax.broadcasted_iota(jnp.int32, (SUBLANES, cbk), 0)

    def tile_scan(a, u, h_in, backward):
        for s in (1, 2, 4):
            sh = SUBLANES - s if backward else s
            a_s, u_s = pltpu.roll(a, sh, 0), pltpu.roll(u, sh, 0)
            m = (rowi < SUBLANES - s) if backward else (rowi >= s)
            u = jnp.where(m, u + a * u_s, u)
            a = jnp.where(m, a * a_s, a)
        h = a * h_in + u
        return h, (h[0:1, :] if backward else h[SUBLANES - 1:SUBLANES, :])

    def body(i, carry):
        hf, hb = carry
        of = pl.multiple_of(i * SUBLANES, SUBLANES)
        h, hf = tile_scan(af[pl.ds(of, SUBLANES), :], uf[pl.ds(of, SUBLANES), :], hf, False)
        af[pl.ds(of, SUBLANES), :] = h
        ob = pl.multiple_of(t_len - SUBLANES - i * SUBLANES, SUBLANES)
        h, hb = tile_scan(ab[pl.ds(ob, SUBLANES), :], ub[pl.ds(ob, SUBLANES), :], hb, True)
        ab[pl.ds(ob, SUBLANES), :] = h
        return hf, hb

    hf, hb = lax.fori_loop(0, t_len // SUBLANES, body, (h0_ref[0, 0:1, :], h0_ref[0, 1:2, :]), unroll=8)
    o_ref[0] = af[...] + ab[...]
    hfin_ref[0, 0:1, :] = hf
    hfin_ref[0, 1:2, :] = hb


def _lru(z3, h0, lw, w_a, w_b):
    b, t, _ = z3.shape
    cbk = CB_LRU
    ncb = w_b // cbk
    off = 4 * w_a // cbk
    ch = min(CH_LRU, t)
    wcol = lambda rows: pl.BlockSpec((rows, cbk), lambda bb, c: (0, c))
    wgate = pl.BlockSpec((2, 1, cbk, cbk), lambda bb, c: (0, c, 0, 0))
    st = pl.BlockSpec((1, 2, cbk), lambda bb, c: (bb, 0, c))
    return pl.pallas_call(
        functools.partial(_lru_kernel, t_len=t, ch=ch),
        grid=(b, ncb),
        in_specs=[pl.BlockSpec((1, t, cbk), lambda bb, c: (bb, 0, off + c)), st,
                  wcol(4), wcol(1), wgate, wcol(2), wgate, wcol(2), wcol(2)],
        out_specs=[pl.BlockSpec((1, t, cbk), lambda bb, c: (bb, 0, c)), st],
        out_shape=[jax.ShapeDtypeStruct((b, t, w_b), F32), jax.ShapeDtypeStruct((b, 2, w_b), F32)],
        scratch_shapes=[pltpu.VMEM((t + 2 * SUBLANES, cbk), F32)] + [pltpu.VMEM((t, cbk), F32)] * 4,
        compiler_params=_params("arbitrary", "arbitrary"),
        cost_estimate=_cost(b * t * w_b * (8 * cbk + 150), 8 * b * t * w_b,
                            *[jax.ShapeDtypeStruct((b, t, w_b), F32)] * 2),
        name="lru",
    )(z3, h0, lw["conv_w"], lw["conv_b"], lw["gr_bd"], lw["gr_b"], lw["gi_bd"], lw["gi_b"], lw["lam"])


def _out_kernel(y_ref, ga_ref, h_ref, gb_ref, ma_ref, mb_ref, x_ref, mod_ref, woa_ref, wob_ref, wo_ref,
                fg_ref, o_ref, *, final):
    ya = jnp.dot((y_ref[...] * _silu(ga_ref[...])).astype(BF16), woa_ref[...], preferred_element_type=F32)
    yb = jnp.dot((h_ref[...] * _silu(gb_ref[...])).astype(BF16), wob_ref[...], preferred_element_type=F32)
    merged = _sigmoid(ma_ref[...]) * ya + _sigmoid(mb_ref[...]) * yb
    out = jnp.dot(merged.astype(BF16), wo_ref[...], preferred_element_type=F32)
    xn = x_ref[...] + mod_ref[0, 2:3, :] * out
    if final:
        xn = xn * lax.rsqrt(jnp.mean(xn * xn, axis=-1, keepdims=True) + RMS_EPS) * fg_ref[...]
    o_ref[...] = xn


def _out(y2, z2, h2, x2, mod, lw, final_g, tm, mod_row, final, w_a, w_b):
    m, d = x2.shape
    layer = lw["layer"]
    once = dict(pipeline_mode=pl.Buffered(1))
    zoff_ga = 3 * w_a // w_a
    zoff_gb = (4 * w_a + w_b) // w_b
    zoff_ma = (4 * w_a + 2 * w_b) // d
    return pl.pallas_call(
        functools.partial(_out_kernel, final=final),
        grid=(m // tm,),
        in_specs=[
            pl.BlockSpec((tm, w_a), lambda i: (i, 0)),
            pl.BlockSpec((tm, w_a), lambda i: (i, zoff_ga)),
            pl.BlockSpec((tm, w_b), lambda i: (i, 0)),
            pl.BlockSpec((tm, w_b), lambda i: (i, zoff_gb)),
            pl.BlockSpec((tm, d), lambda i: (i, zoff_ma)),
            pl.BlockSpec((tm, d), lambda i: (i, zoff_ma + 1)),
            pl.BlockSpec((tm, d), lambda i: (i, 0)),
            pl.BlockSpec((1, 3, d), lambda i: (mod_row(i), 0, 0)),
            pl.BlockSpec((None, w_a, d), lambda i: (layer, 0, 0), **once),
            pl.BlockSpec((None, w_b, d), lambda i: (layer, 0, 0), **once),
            pl.BlockSpec((None, d, d), lambda i: (layer, 0, 0), **once),
            pl.BlockSpec((1, d), lambda i: (0, 0)),
        ],
        out_specs=pl.BlockSpec((tm, d), lambda i: (i, 0)),
        out_shape=jax.ShapeDtypeStruct((m, d), F32),
        compiler_params=_params("arbitrary"),
        cost_estimate=_cost(2 * m * d * (w_a + w_b + d), 4 * m * d, y2, y2, h2, h2, x2, x2, x2, x2, x2),
        name="out_final" if final else "out",
    )(y2, z2, h2, z2, z2, z2, x2, mod, lw["w_out_a"], lw["w_out_b"], lw["w_out"], final_g.reshape(1, d))


def _to_chains(x, heads):
    b, t, _ = x.shape
    nt = b // CHAIN_BATCH
    y = jnp.swapaxes(x.reshape(nt, CHAIN_BATCH, t, heads * HEAD), 1, 2)
    return jnp.swapaxes(y.reshape(nt, t, CHAIN_BATCH * heads, HEAD), 2, 3)


def _from_chains(x, heads):
    nt, t = x.shape[0], x.shape[1]
    y = jnp.swapaxes(x, 2, 3).reshape(nt, t, CHAIN_BATCH, heads * HEAD)
    return jnp.swapaxes(y, 1, 2).reshape(nt * CHAIN_BATCH, t, heads * HEAD)


def _chain_const(p, heads):
    t = p.reshape(heads, HEAD).T
    return jnp.broadcast_to(t[:, None, :], (HEAD, CHAIN_BATCH, heads)).reshape(HEAD, CHAIN_BATCH * heads)


def _state_to_chains(s):
    b, h = s.shape[0], s.shape[1]
    nt = b // CHAIN_BATCH
    s = s.reshape(nt, CHAIN_BATCH, h, HEAD, HEAD)
    return s.transpose(0, 4, 3, 1, 2).reshape(nt, HEAD, HEAD, CHAIN_BATCH * h)


def _states_from_chains(s, heads):
    n_l, n_d, nt = s.shape[:3]
    s = s.reshape(n_l, n_d, nt, HEAD, HEAD, CHAIN_BATCH, heads)
    return s.transpose(2, 5, 0, 1, 6, 4, 3).reshape(nt * CHAIN_BATCH, n_l, n_d, heads, HEAD, HEAD)


def _block_diag(w, per):
    h, n, _ = w.shape
    g = h // per
    eye = jnp.eye(per, dtype=w.dtype)
    wg = w.reshape(g, per, n, n)
    return (wg[:, :, :, None, :] * eye[None, :, None, :, None]).reshape(g, per * n, per * n)


def _layer_weights(l, n_layers, heads, w_in, mu_rkv, dec_w0, dec_w1, dec_w2, iclr_w0, iclr_w1, iclr_w2,
                   vres_w0, vres_w1, vres_w2, k_k, k_a, r_k, lnx_g, lnx_b, w_out_a, conv_w, conv_b, gr_w,
                   gr_b, gi_w, gi_b, lru_lambda, w_out_b, w_out):
    d = w_in.shape[1]
    w_a = heads * HEAD
    r_dec, r_iclr, r_vres = dec_w1.shape[-1], iclr_w1.shape[-1], vres_w1.shape[-1]
    assert (r_dec, r_iclr, r_vres) == (96, 96, 64), "LoRA column layout assumes ranks 96/96/64"
    has_vres = l > 0
    vw1 = vres_w1[l - 1] if has_vres else jnp.zeros((d, r_vres), F32)
    used = 2 * r_dec + 2 * r_iclr + r_vres
    w_lora = jnp.concatenate(
        [dec_w1[l, 0], dec_w1[l, 1], iclr_w1[l, 0], iclr_w1[l, 1], vw1,
         jnp.zeros((d, LORA_COLS - used), F32)], axis=1).astype(BF16)
    dec_w2p = jnp.zeros((2, 256, w_a), F32)
    iclr_w2p = jnp.zeros((2, 256, w_a), F32)
    for dd in range(2):
        dec_w2p = dec_w2p.at[dd, r_dec * dd:r_dec * (dd + 1)].set(dec_w2[l, dd])
        o = 2 * r_dec - 128 + r_iclr * dd
        iclr_w2p = iclr_w2p.at[dd, o:o + r_iclr].set(iclr_w2[l, dd])
    lw = dict(
        layer=l, w_main=w_in, w_lora=w_lora, mu=mu_rkv[l], dec_w0=dec_w0[l], dec_w2p=dec_w2p.astype(BF16), iclr_w0=iclr_w0[l],
        iclr_w2p=iclr_w2p.astype(BF16),
        kk_c=_chain_const(k_k[l], heads), ka_c=_chain_const(k_a[l], heads),
        rk_c=_chain_const(r_k[l].reshape(-1), heads), g_c=_chain_const(lnx_g[l], heads),
        b_c=_chain_const(lnx_b[l], heads),
        conv_w=conv_w[l], conv_b=conv_b[l].reshape(1, -1),
        gr_bd=jnp.stack([_block_diag(gr_w[l, dd], CB_LRU // gr_w.shape[-1]) for dd in range(2)]).astype(BF16),
        gi_bd=jnp.stack([_block_diag(gi_w[l, dd], CB_LRU // gi_w.shape[-1]) for dd in range(2)]).astype(BF16),
        gr_b=gr_b[l], gi_b=gi_b[l], lam=lru_lambda[l],
        w_out_a=w_out_a, w_out_b=w_out_b, w_out=w_out,
    )
    if has_vres:
        vres_w2p = jnp.zeros((128, w_a), F32).at[0:r_vres].set(vres_w2[l - 1])
        lw.update(vres_w0=vres_w0[l - 1].reshape(1, -1), vres_w2p=vres_w2p.astype(BF16))
    return lw


def _trunk(x, mods, mod_row_in, mod_row_out, tm_in, tm_out, grid_w, rwkv_init, lru_init, layers, final_g,
           heads, w_b):
    b, t, d = x.shape
    w_a = heads * HEAD
    n_layers = len(layers)
    x2 = x.reshape(b * t, d)
    v_first = None
    s_out, h_out = [], []
    for l, lw in enumerate(layers):
        z2, zl = _inproj(x2, mods[l], lw["norm_g"], lw["w_main"], l, lw["w_lora"], tm_in, mod_row_in)
        z3 = z2.reshape(b, t, z2.shape[1])
        r, k, v, lw_f, lw_bk, a_f, a_bk = _prep(z3, zl.reshape(b, t, LORA_COLS), v_first, lw, grid_w, w_a)
        if v_first is None:
            v_first = v
        h_sum, h_fin = _lru(z3, lru_init[l], lw, w_a, w_b)
        rc, kc, vc = (_to_chains(q, heads) for q in (r, k, v))
        consts = (lw["kk_c"], lw["ka_c"], lw["rk_c"])
        y_f, bo_f, s_f = _scan(rc, kc, vc, _to_chains(lw_f, heads), _to_chains(a_f, heads), *consts,
                               rwkv_init[l][0], False)
        y_mix, _, s_b = _scan(rc, kc, vc, _to_chains(lw_bk, heads), _to_chains(a_bk, heads), *consts,
                              rwkv_init[l][1], True, merge=(y_f, bo_f, lw["g_c"], lw["b_c"]))
        y_rwkv = _from_chains(y_mix, heads)
        x2 = _out(y_rwkv.reshape(b * t, w_a), z2, h_sum.reshape(b * t, w_b), x2, mods[l], lw, final_g,
                  tm_out, mod_row_out, l == n_layers - 1, w_a, w_b)
        s_out.append((s_f, s_b))
        h_out.append(h_fin)
    return x2.reshape(b, t, d), s_out, h_out


def kernel(x_prompt, x_sample, c, state_rwkv, state_lru, c_ctx, norm_g, ada_w, ada_b, w_in, mu_rkv, dec_w0,
           dec_w1, dec_w2, iclr_w0, iclr_w1, iclr_w2, vres_w0, vres_w1, vres_w2, k_k, k_a, r_k, lnx_g, lnx_b,
           w_out_a, conv_w, conv_b, gr_w, gr_b, gi_w, gi_b, lru_lambda, w_out_b, w_out, final_g):
    n_layers, d = norm_g.shape
    bp, tp, _ = x_prompt.shape
    bs, ts, _ = x_sample.shape
    heads = r_k.shape[1]
    w_b = conv_w.shape[-1]
    grid_w = int(round(ts ** 0.5))
    assert grid_w * grid_w == ts and heads * CHAIN_BATCH == LANES
    assert bp % CHAIN_BATCH == 0 and bs % CHAIN_BATCH == 0

    w_in, w_out_a, w_out_b, w_out = (w.astype(BF16) for w in (w_in, w_out_a, w_out_b, w_out))
    layers = []
    for l in range(n_layers):
        lw = _layer_weights(l, n_layers, heads, w_in, mu_rkv, dec_w0, dec_w1, dec_w2, iclr_w0, iclr_w1,
                            iclr_w2, vres_w0, vres_w1, vres_w2, k_k, k_a, r_k, lnx_g, lnx_b, w_out_a, conv_w,
                            conv_b, gr_w, gr_b, gi_w, gi_b, lru_lambda, w_out_b, w_out)
        lw["norm_g"] = norm_g[l]
        layers.append(lw)

    ctx_row = bs
    n_rows = -(-(bs + 1) // SUBLANES) * SUBLANES
    cond = jnp.zeros((n_rows, d), F32).at[0:bs].set(c).at[ctx_row].set(c_ctx)
    mod = _ada_mod(cond, ada_w, ada_b).reshape(n_layers, n_rows, 3, d)
    mods = [mod[l] for l in range(n_layers)]

    zero_s = jnp.zeros((bp // CHAIN_BATCH, HEAD, HEAD, LANES), F32)
    tm_p = min(TM_IN, bp * tp)
    tm_po = min(TM_OUT, bp * tp)
    x_ctx, s_ctx, h_ctx = _trunk(
        x_prompt, mods, lambda i: ctx_row, lambda i: ctx_row, tm_p, tm_po, 0,
        [(zero_s, zero_s)] * n_layers, [jnp.zeros((bp, 2, w_b), F32)] * n_layers, layers, final_g, heads, w_b)

    tm_s = min(TM_IN, ts)
    tm_so = min(TM_OUT, ts)
    assert ts % tm_s == 0 and ts % tm_so == 0
    rwkv_init = [tuple(_state_to_chains(state_rwkv[:, l, dd]) for dd in range(2)) for l in range(n_layers)]
    x_lat, _, _ = _trunk(
        x_sample, mods, lambda i: i // (ts // tm_s), lambda i: i // (ts // tm_so), tm_s, tm_so, grid_w,
        rwkv_init, [state_lru[:, l] for l in range(n_layers)], layers, final_g, heads, w_b)

    new_rwkv = _states_from_chains(
        jnp.stack([jnp.stack([s_ctx[l][dd] for dd in range(2)]) for l in range(n_layers)]), heads)
    new_lru = jnp.stack(h_ctx, axis=1)
    return x_ctx, x_lat, new_rwkv, new_lru
```

```python
import functools
import math

import jax
import jax.numpy as jnp
from jax import lax
from jax.experimental import pallas as pl
from jax.experimental.pallas import tpu as pltpu

F32 = jnp.float32
BF16 = jnp.bfloat16

LANES = 128
SUBLANES = 8
HEAD = 64
CHAIN_BATCH = 8
LORA_COLS = 512
RMS_EPS = 1e-6
GN_EPS = 64e-5
LRU_C = 8.0
KK_EPS = 1e-12
DECAY_SCALE = math.exp(-0.5) * math.log2(math.e)
VMEM_LIMIT = 56 * 1024 * 1024

TM_IN = 1024
TN_IN = 1536
ROWS_NORM = 16
TM_OUT = 256
TT_PREP = 512
ROWS_PREP = 64
TILE_PREP = 512 * 512
TB_SCAN = 64
ROWS_SCAN = 64
CB_LRU = 256
CH_LRU = 512


def _params(*sem):
    return pltpu.CompilerParams(dimension_semantics=sem, vmem_limit_bytes=VMEM_LIMIT)


def _cost(flops, transcendentals, *arrays):
    nbytes = sum(math.prod(a.shape) * jnp.dtype(a.dtype).itemsize for a in arrays)
    return pl.CostEstimate(flops=int(flops), transcendentals=int(transcendentals), bytes_accessed=int(nbytes))


def _softplus(x):
    return jnp.maximum(x, 0.0) + jnp.log1p(jnp.exp(-jnp.abs(x)))


def _sigmoid(x):
    return 0.5 * jnp.tanh(0.5 * x) + 0.5


def _silu(x):
    return x * _sigmoid(x)


def _ada_kernel(c_ref, w_ref, b_ref, o_ref):
    h = _silu(c_ref[...]).astype(BF16)
    o_ref[0] = jnp.dot(h, w_ref[0].astype(BF16), preferred_element_type=F32) + b_ref[0]


def _ada_mod(cond, ada_w, ada_b):
    n_l, d, n = ada_w.shape
    bc = cond.shape[0]
    tn = math.gcd(n, 512)
    return pl.pallas_call(
        _ada_kernel,
        grid=(n_l, n // tn),
        in_specs=[
            pl.BlockSpec((bc, d), lambda l, j: (0, 0)),
            pl.BlockSpec((1, d, tn), lambda l, j: (l, 0, j)),
            pl.BlockSpec((1, 1, tn), lambda l, j: (l, 0, j)),
        ],
        out_specs=pl.BlockSpec((1, bc, tn), lambda l, j: (l, 0, j)),
        out_shape=jax.ShapeDtypeStruct((n_l, bc, n), F32),
        compiler_params=_params("arbitrary", "arbitrary"),
        name="ada",
    )(cond, ada_w, ada_b.reshape(n_l, 1, n))


def _inproj_kernel(x_ref, mod_ref, g_ref, w_ref, wl_ref, o_ref, ol_ref, xm_ref):
    j = pl.program_id(1)

    @pl.when(j == 0)
    def _():
        gain = g_ref[...]
        scale = 1.0 + mod_ref[0, 1:2, :]
        shift = mod_ref[0, 0:1, :]

        def rows_body(c, carry):
            rows = pl.ds(pl.multiple_of(c * ROWS_NORM, ROWS_NORM), ROWS_NORM)
            x = x_ref[rows, :]
            y = x * lax.rsqrt(jnp.mean(x * x, axis=-1, keepdims=True) + RMS_EPS) * gain
            xm_ref[rows, :] = (y * scale + shift).astype(BF16)
            return carry

        lax.fori_loop(0, x_ref.shape[0] // ROWS_NORM, rows_body, 0, unroll=8)

    o_ref[...] = jnp.dot(xm_ref[...], w_ref[...], preferred_element_type=F32)

    @pl.when(j == pl.num_programs(1) - 1)
    def _():
        ol_ref[...] = jnp.dot(xm_ref[...], wl_ref[...], preferred_element_type=F32)


def _col_tile(n, cap):
    return 256 * max(q for q in range(1, cap // 256 + 1) if (n // 256) % q == 0)


def _inproj(x2, mod, norm_g, w_main, layer, w_lora, tm, mod_row):
    m, d = x2.shape
    n = w_main.shape[2]
    nl = w_lora.shape[1]
    tn = _col_tile(n, TN_IN)
    outs = [jax.ShapeDtypeStruct((m, n), F32), jax.ShapeDtypeStruct((m, nl), F32)]
    return pl.pallas_call(
        _inproj_kernel,
        grid=(m // tm, n // tn),
        in_specs=[
            pl.BlockSpec((tm, d), lambda i, j: (i, 0)),
            pl.BlockSpec((1, 3, d), lambda i, j: (mod_row(i), 0, 0)),
            pl.BlockSpec((1, d), lambda i, j: (0, 0)),
            pl.BlockSpec((None, d, tn), lambda i, j: (layer, 0, j)),
            pl.BlockSpec((d, nl), lambda i, j: (0, 0)),
        ],
        out_specs=[pl.BlockSpec((tm, tn), lambda i, j: (i, j)), pl.BlockSpec((tm, nl), lambda i, j: (i, 0))],
        out_shape=outs,
        scratch_shapes=[pltpu.VMEM((tm, d), BF16)],
        compiler_params=_params("arbitrary", "arbitrary"),
        cost_estimate=_cost(2 * m * d * (n + nl), m, x2, jax.ShapeDtypeStruct((d, n), BF16), w_lora, *outs),
        name="inproj",
    )(x2, mod, norm_g.reshape(1, d), w_main, w_lora)


def _prep_kernel(*refs, tt, cw, grid_w, n_t, has_vres):
    it = iter(refs)
    zr, zk, zv = next(it), next(it), next(it)
    halos = [next(it) for _ in range(6)] if grid_w else [None] * 6
    lo_ref = next(it)
    vfirst = next(it) if has_vres else None
    mu, dw0, dw2, iw0, iw2 = (next(it) for _ in range(5))
    vw0, vw2 = (next(it), next(it)) if has_vres else (None, None)
    r_o, k_o, v_o, wf_o, wb_o, af_o, ab_o = (next(it) for _ in range(7))
    td_s, li_s, lv_s, er_s, ek_s, ev_s = (next(it) for _ in range(6))

    @pl.when(pl.program_id(2) == 0)
    def _():
        td_s[...] = jnp.tanh(lo_ref[0, :, 0:256]).astype(BF16)
        li_s[...] = lo_ref[0, :, 128:384].astype(BF16)
        lv_s[...] = lo_ref[0, :, 384:512].astype(BF16)

    pad = grid_w if grid_w else SUBLANES
    for e_s, z_ref, hp, hn in ((er_s, zr, halos[0], halos[1]), (ek_s, zk, halos[2], halos[3]),
                               (ev_s, zv, halos[4], halos[5])):
        if grid_w:
            i = pl.program_id(1)
            e_s[0:pad, :] = jnp.where(i > 0, hp[0], 0.0)
            e_s[pad + tt:2 * pad + tt, :] = jnp.where(i < n_t - 1, hn[0], 0.0)
        else:
            e_s[0:pad, :] = jnp.zeros((pad, cw), F32)
            e_s[pad + tt:2 * pad + tt, :] = jnp.zeros((pad, cw), F32)
        e_s[pad:pad + tt, :] = z_ref[0]

    rc = min(tt, ROWS_PREP)
    col = lax.rem(lax.broadcasted_iota(jnp.int32, (rc, cw), 0), grid_w) if grid_w else None

    def mix(e_s, c0, j):
        z = e_s[pad + c0:pad + c0 + rc, :]
        before = e_s[pad + c0 - 1:pad + c0 - 1 + rc, :]
        after = e_s[pad + c0 + 1:pad + c0 + 1 + rc, :]
        if grid_w:
            up = e_s[c0:c0 + rc, :]
            down = e_s[2 * pad + c0:2 * pad + c0 + rc, :]
            left = jnp.where(col == 0, 0.0, before)
            right = jnp.where(col == grid_w - 1, 0.0, after)
            sh = 0.25 * (up + down + left + right)
        else:
            sh = 0.5 * (before + after)
        return z + (sh - z) * mu[j:j + 1, :]

    for c0 in range(0, tt, rc):
        rows = pl.ds(c0, rc)
        r_o[0, rows, :] = mix(er_s, c0, 0)
        k_o[0, rows, :] = mix(ek_s, c0, 1)
        v = mix(ev_s, c0, 2)
        if has_vres:
            gate = _sigmoid(vw0[...] + jnp.dot(lv_s[rows, :], vw2[...], preferred_element_type=F32))
            v = v + (vfirst[0, rows, :] - v) * gate
        v_o[0, rows, :] = v
        td = td_s[rows, :]
        li = li_s[rows, :]
        for d, (w_o, a_o) in enumerate(((wf_o, af_o), (wb_o, ab_o))):
            xd = dw0[d:d + 1, :] + jnp.dot(td, dw2[d], preferred_element_type=F32)
            w_o[0, rows, :] = -DECAY_SCALE * _sigmoid(xd)
            a_o[0, rows, :] = _sigmoid(iw0[d:d + 1, :] + jnp.dot(li, iw2[d], preferred_element_type=F32))


def _prep(z3, zl3, v_first, lw, grid_w, w_a):
    b, t, _ = z3.shape
    has_vres = v_first is not None
    if grid_w:
        tt = min(TT_PREP, t)
        assert min(tt, ROWS_PREP) % grid_w == 0 and grid_w % SUBLANES == 0 and t % tt == 0
    else:
        tt = t
    n_t = t // tt
    cw = min(w_a, max(LANES, TILE_PREP // tt // LANES * LANES))
    assert w_a % cw == 0
    nc = w_a // cw
    blk = lambda off: pl.BlockSpec((1, tt, cw), lambda bb, i, c, off=off: (bb, i, off + c))
    in_specs = [blk(0), blk(nc), blk(2 * nc)]
    args = [z3, z3, z3]
    if grid_w:
        per, last = tt // grid_w, t // grid_w - 1
        for off in (0, nc, 2 * nc):
            in_specs.append(pl.BlockSpec(
                (1, grid_w, cw),
                lambda bb, i, c, off=off: (bb, jnp.maximum(i * per - 1, 0), off + c)))
            in_specs.append(pl.BlockSpec(
                (1, grid_w, cw),
                lambda bb, i, c, off=off: (bb, jnp.minimum((i + 1) * per, last), off + c)))
            args += [z3, z3]
    in_specs.append(pl.BlockSpec((1, tt, LORA_COLS), lambda bb, i, c: (bb, i, 0)))
    args.append(zl3)
    if has_vres:
        in_specs.append(blk(0))
        args.append(v_first)
    wcol = lambda shape: pl.BlockSpec(shape, lambda bb, i, c: (0,) * (len(shape) - 1) + (c,))
    in_specs += [wcol((3, cw)), wcol((2, cw)), wcol((2, 256, cw)), wcol((2, cw)), wcol((2, 256, cw))]
    args += [lw["mu"], lw["dec_w0"], lw["dec_w2p"], lw["iclr_w0"], lw["iclr_w2p"]]
    if has_vres:
        in_specs += [wcol((1, cw)), wcol((128, cw))]
        args += [lw["vres_w0"], lw["vres_w2p"]]
    out = jax.ShapeDtypeStruct((b, t, w_a), F32)
    return pl.pallas_call(
        functools.partial(_prep_kernel, tt=tt, cw=cw, grid_w=grid_w, n_t=n_t, has_vres=has_vres),
        grid=(b, n_t, nc),
        in_specs=in_specs,
        out_specs=[blk(0)] * 7,
        out_shape=[out] * 7,
        scratch_shapes=[pltpu.VMEM((tt, 256), BF16), pltpu.VMEM((tt, 256), BF16), pltpu.VMEM((tt, 128), BF16)]
        + [pltpu.VMEM((tt + 2 * (grid_w if grid_w else SUBLANES), cw), F32)] * 3,
        compiler_params=_params("arbitrary", "arbitrary", "arbitrary"),
        cost_estimate=_cost(b * t * w_a * (2 * 5 * 256 + 60), 6 * b * t * w_a, *[out] * 11),
        name="prep",
    )(*args)


def _scan_kernel(*refs, tb, reverse, merge):
    it = iter(refs)
    r_ref, k_ref, v_ref, lw_ref, a_ref, kkc_ref, kac_ref, rkc_ref, s0_ref = (next(it) for _ in range(9))
    yo_ref, boo_ref, g_ref, b_ref = (next(it) for _ in range(4)) if merge else (None,) * 4
    y_ref, bo_ref, sfin_ref, s_scr, al_s, be_s, kt_s, rt_s = (next(it) for _ in range(8))
    assert not merge or ROWS_SCAN == HEAD, "the merged group norm needs a whole head per row group"
    i = pl.program_id(1)

    @pl.when(i == 0)
    def _():
        s_scr[...] = s0_ref[0]

    off = 1 if reverse else 0
    al_s[0 if reverse else tb] = jnp.zeros((HEAD, LANES), F32)

    kac = kac_ref[...]
    one_minus_kac = 1.0 - kac

    def prep_step(s, carry):
        lcum, wprev = carry
        t = (tb - 1 - s) if reverse else s
        k_t, a_t, r_t = k_ref[0, t], a_ref[0, t], r_ref[0, t]
        kkr = k_t * kkc_ref[...]
        nrm = jnp.sqrt(jnp.sum(kkr * kkr, axis=0, keepdims=True))
        kk = kkr * (1.0 / jnp.maximum(nrm, KK_EPS))
        al_s[t + off] = wprev * kk
        lcum = lcum + lw_ref[0, t]
        wcum = jnp.exp2(lcum)
        inv = jnp.exp2(-lcum)
        kd = k_t * (a_t * kac + one_minus_kac)
        be_s[t] = (kk * a_t) * inv
        kt_s[t] = kd * inv
        rt_s[t] = wcum * r_t
        bo = jnp.sum((r_t * kd) * rkc_ref[...], axis=0, keepdims=True)
        bo_ref[0, t] = jnp.broadcast_to(bo, (SUBLANES, LANES))
        return lcum, wcum

    _, w_last = lax.fori_loop(0, tb, prep_step, (jnp.zeros((HEAD, LANES), F32), jnp.ones((HEAD, LANES), F32)),
                              unroll=4)

    groups = [pl.ds(g * ROWS_SCAN, ROWS_SCAN) for g in range(HEAD // ROWS_SCAN)]

    def s_dot_al(rows, slot):
        acc = [None, None]
        for kx in range(HEAD):
            term = s_scr[kx, rows, :] * al_s[slot, pl.ds(kx, 1), :]
            acc[kx % 2] = term if acc[kx % 2] is None else acc[kx % 2] + term
        return acc[0] + acc[1]

    def time_of(s):
        return (tb - 1 - s) if reverse else s

    def finish(s):
        t = time_of(s)
        y = yo_ref[0, t] + y_ref[0, t]
        dlt = y - jnp.mean(y, axis=0, keepdims=True)
        var = jnp.mean(dlt * dlt, axis=0, keepdims=True)
        bo = boo_ref[0, t, 0:1, :] + bo_ref[0, t, 0:1, :]
        y_ref[0, t] = dlt * lax.rsqrt(var + GN_EPS) * g_ref[...] + b_ref[...] + bo * v_ref[0, t]

    def step(s, sal_all):
        t = time_of(s)
        nxt = t if reverse else t + 1
        sal_next = []
        for rows, sal in zip(groups, sal_all):
            vv = v_ref[0, t, rows, :]
            yac = acc = None
            for kx in range(HEAD):
                kr = pl.ds(kx, 1)
                sn = s_scr[kx, rows, :] - sal * be_s[t, kr, :] + vv * kt_s[t, kr, :]
                s_scr[kx, rows, :] = sn
                ty = sn * rt_s[t, kr, :]
                ta = sn * al_s[nxt, kr, :]
                yac = ty if yac is None else yac + ty
                acc = ta if acc is None else acc + ta
            y_ref[0, t, rows, :] = yac
            sal_next.append(acc)
        return tuple(sal_next)

    def step_and_finish(s, sal_all):
        finish(s - 1)
        return step(s, sal_all)

    first = (tb - 1 + off) if reverse else 0
    sal0 = tuple(s_dot_al(rows, first) for rows in groups)
    if merge:
        lax.fori_loop(1, tb, step_and_finish, step(0, sal0))
        finish(tb - 1)
    else:
        lax.fori_loop(0, tb, step, sal0)

    wl_s = be_s.at[0]
    wl_s[...] = w_last
    for kx in range(HEAD):
        s_scr[kx] = s_scr[kx] * wl_s[pl.ds(kx, 1), :]

    @pl.when(i == pl.num_programs(1) - 1)
    def _():
        sfin_ref[0] = s_scr[...]


def _scan(r, k, v, lw, a, kk_c, ka_c, rk_c, s0, reverse, merge=()):
    nt, t = r.shape[0], r.shape[1]
    tb = min(TB_SCAN, t)
    nb = t // tb
    tmap = (lambda c, i: (c, nb - 1 - i, 0, 0)) if reverse else (lambda c, i: (c, i, 0, 0))
    seq = pl.BlockSpec((1, tb, HEAD, LANES), tmap)
    row = pl.BlockSpec((1, tb, SUBLANES, LANES), tmap)
    cst = pl.BlockSpec((HEAD, LANES), lambda c, i: (0, 0))
    st = pl.BlockSpec((1, HEAD, HEAD, LANES), lambda c, i: (c, 0, 0, 0))
    return pl.pallas_call(
        functools.partial(_scan_kernel, tb=tb, reverse=reverse, merge=bool(merge)),
        grid=(nt, nb),
        in_specs=[seq] * 5 + [cst, cst, cst, st] + ([seq, row, cst, cst] if merge else []),
        out_specs=[seq, row, st],
        out_shape=[jax.ShapeDtypeStruct(r.shape, F32), jax.ShapeDtypeStruct((nt, t, SUBLANES, LANES), F32),
                   jax.ShapeDtypeStruct(s0.shape, F32)],
        scratch_shapes=[pltpu.VMEM((HEAD, HEAD, LANES), F32), pltpu.VMEM((tb + 1, HEAD, LANES), F32)]
        + [pltpu.VMEM((tb, HEAD, LANES), F32)] * 3,
        compiler_params=_params("arbitrary", "arbitrary"),
        cost_estimate=_cost(8 * r.size * HEAD, 4 * r.size, r, k, v, lw, a, r, s0, s0, *merge[:1]),
        name="scan_bwd" if reverse else "scan_fwd",
    )(r, k, v, lw, a, kk_c, ka_c, rk_c, s0, *merge)


def _lru_kernel(x_ref, h0_ref, cw_ref, cb_ref, wr_ref, br_ref, wi_ref, bi_ref, lam_ref, o_ref, hfin_ref,
                xpad, af, uf, ab, ub, *, t_len, ch):
    cbk = x_ref.shape[-1]
    pad = SUBLANES
    xpad[0:pad, :] = jnp.zeros((pad, cbk), F32)
    xpad[t_len + pad:t_len + 2 * pad, :] = jnp.zeros((pad, cbk), F32)
    xpad[pad:t_len + pad, :] = x_ref[0]
    cw = cw_ref[...]
    for c0 in range(0, t_len, ch):
        xc = cb_ref[...] + xpad[c0 + pad - 2:c0 + pad - 2 + ch, :] * cw[0:1, :]
        for j in range(1, 4):
            xc = xc + xpad[c0 + pad - 2 + j:c0 + pad - 2 + j + ch, :] * cw[j:j + 1, :]
        xcb = xc.astype(BF16)
        for d, (a_s, u_s) in enumerate(((af, uf), (ab, ub))):
            rg = _sigmoid(jnp.dot(xcb, wr_ref[d, 0], preferred_element_type=F32) + br_ref[d:d + 1, :])
            ig = _sigmoid(jnp.dot(xcb, wi_ref[d, 0], preferred_element_type=F32) + bi_ref[d:d + 1, :])
            log_a = (-LRU_C * _softplus(-lam_ref[d:d + 1, :])) * rg
            a = jnp.exp(log_a)
            a_s[c0:c0 + ch, :] = a
            s2 = 1.0 - a * a
            root = jnp.where(s2 > 0.0, s2 * lax.rsqrt(s2), 0.0)
            u_s[c0:c0 + ch, :] = root * (ig * xc)

    rowi = lax.broadcasted_iota(jnp.int32, (SUBLANES, cbk), 0)

    def tile_scan(a, u, h_in, backward):
        for s in (1, 2, 4):
            sh = SUBLANES - s if backward else s
            a_s, u_s = pltpu.roll(a, sh, 0), pltpu.roll(u, sh, 0)
            m = (rowi < SUBLANES - s) if backward else (rowi >= s)
            u = jnp.where(m, u + a * u_s, u)
            a = jnp.where(m, a * a_s, a)
        h = a * h_in + u
        return h, (h[0:1, :] if backward else h[SUBLANES - 1:SUBLANES, :])

    def body(i, carry):
        hf, hb = carry
        of = pl.multiple_of(i * SUBLANES, SUBLANES)
        h, hf = tile_scan(af[pl.ds(of, SUBLANES), :], uf[pl.ds(of, SUBLANES), :], hf, False)
        af[pl.ds(of, SUBLANES), :] = h
        ob = pl.multiple_of(t_len - SUBLANES - i * SUBLANES, SUBLANES)
        h, hb = tile_scan(ab[pl.ds(ob, SUBLANES), :], ub[pl.ds(ob, SUBLANES), :], hb, True)
        ab[pl.ds(ob, SUBLANES), :] = h
        return hf, hb

    hf, hb = lax.fori_loop(0, t_len // SUBLANES, body, (h0_ref[0, 0:1, :], h0_ref[0, 1:2, :]), unroll=8)
    o_ref[0] = af[...] + ab[...]
    hfin_ref[0, 0:1, :] = hf
    hfin_ref[0, 1:2, :] = hb


def _lru(z3, h0, lw, w_a, w_b):
    b, t, _ = z3.shape
    cbk = CB_LRU
    ncb = w_b // cbk
    off = 4 * w_a // cbk
    ch = min(CH_LRU, t)
    wcol = lambda rows: pl.BlockSpec((rows, cbk), lambda bb, c: (0, c))
    wgate = pl.BlockSpec((2, 1, cbk, cbk), lambda bb, c: (0, c, 0, 0))
    st = pl.BlockSpec((1, 2, cbk), lambda bb, c: (bb, 0, c))
    return pl.pallas_call(
        functools.partial(_lru_kernel, t_len=t, ch=ch),
        grid=(b, ncb),
        in_specs=[pl.BlockSpec((1, t, cbk), lambda bb, c: (bb, 0, off + c)), st,
                  wcol(4), wcol(1), wgate, wcol(2), wgate, wcol(2), wcol(2)],
        out_specs=[pl.BlockSpec((1, t, cbk), lambda bb, c: (bb, 0, c)), st],
        out_shape=[jax.ShapeDtypeStruct((b, t, w_b), F32), jax.ShapeDtypeStruct((b, 2, w_b), F32)],
        scratch_shapes=[pltpu.VMEM((t + 2 * SUBLANES, cbk), F32)] + [pltpu.VMEM((t, cbk), F32)] * 4,
        compiler_params=_params("arbitrary", "arbitrary"),
        cost_estimate=_cost(b * t * w_b * (8 * cbk + 150), 8 * b * t * w_b,
                            *[jax.ShapeDtypeStruct((b, t, w_b), F32)] * 2),
        name="lru",
    )(z3, h0, lw["conv_w"], lw["conv_b"], lw["gr_bd"], lw["gr_b"], lw["gi_bd"], lw["gi_b"], lw["lam"])


def _out_kernel(y_ref, ga_ref, h_ref, gb_ref, ma_ref, mb_ref, x_ref, mod_ref, woa_ref, wob_ref, wo_ref,
                fg_ref, o_ref, *, final):
    ya = jnp.dot((y_ref[...] * _silu(ga_ref[...])).astype(BF16), woa_ref[...], preferred_element_type=F32)
    yb = jnp.dot((h_ref[...] * _silu(gb_ref[...])).astype(BF16), wob_ref[...], preferred_element_type=F32)
    merged = _sigmoid(ma_ref[...]) * ya + _sigmoid(mb_ref[...]) * yb
    out = jnp.dot(merged.astype(BF16), wo_ref[...], preferred_element_type=F32)
    xn = x_ref[...] + mod_ref[0, 2:3, :] * out
    if final:
        xn = xn * lax.rsqrt(jnp.mean(xn * xn, axis=-1, keepdims=True) + RMS_EPS) * fg_ref[...]
    o_ref[...] = xn


def _out(y2, z2, h2, x2, mod, lw, final_g, tm, mod_row, final, w_a, w_b):
    m, d = x2.shape
    layer = lw["layer"]
    once = dict(pipeline_mode=pl.Buffered(1))
    zoff_ga = 3 * w_a // w_a
    zoff_gb = (4 * w_a + w_b) // w_b
    zoff_ma = (4 * w_a + 2 * w_b) // d
    return pl.pallas_call(
        functools.partial(_out_kernel, final=final),
        grid=(m // tm,),
        in_specs=[
            pl.BlockSpec((tm, w_a), lambda i: (i, 0)),
            pl.BlockSpec((tm, w_a), lambda i: (i, zoff_ga)),
            pl.BlockSpec((tm, w_b), lambda i: (i, 0)),
            pl.BlockSpec((tm, w_b), lambda i: (i, zoff_gb)),
            pl.BlockSpec((tm, d), lambda i: (i, zoff_ma)),
            pl.BlockSpec((tm, d), lambda i: (i, zoff_ma + 1)),
            pl.BlockSpec((tm, d), lambda i: (i, 0)),
            pl.BlockSpec((1, 3, d), lambda i: (mod_row(i), 0, 0)),
            pl.BlockSpec((None, w_a, d), lambda i: (layer, 0, 0), **once),
            pl.BlockSpec((None, w_b, d), lambda i: (layer, 0, 0), **once),
            pl.BlockSpec((None, d, d), lambda i: (layer, 0, 0), **once),
            pl.BlockSpec((1, d), lambda i: (0, 0)),
        ],
        out_specs=pl.BlockSpec((tm, d), lambda i: (i, 0)),
        out_shape=jax.ShapeDtypeStruct((m, d), F32),
        compiler_params=_params("arbitrary"),
        cost_estimate=_cost(2 * m * d * (w_a + w_b + d), 4 * m * d, y2, y2, h2, h2, x2, x2, x2, x2, x2),
        name="out_final" if final else "out",
    )(y2, z2, h2, z2, z2, z2, x2, mod, lw["w_out_a"], lw["w_out_b"], lw["w_out"], final_g.reshape(1, d))


def _to_chains(x, heads):
    b, t, _ = x.shape
    nt = b // CHAIN_BATCH
    y = jnp.swapaxes(x.reshape(nt, CHAIN_BATCH, t, heads * HEAD), 1, 2)
    return jnp.swapaxes(y.reshape(nt, t, CHAIN_BATCH * heads, HEAD), 2, 3)


def _from_chains(x, heads):
    nt, t = x.shape[0], x.shape[1]
    y = jnp.swapaxes(x, 2, 3).reshape(nt, t, CHAIN_BATCH, heads * HEAD)
    return jnp.swapaxes(y, 1, 2).reshape(nt * CHAIN_BATCH, t, heads * HEAD)


def _chain_const(p, heads):
    t = p.reshape(heads, HEAD).T
    return jnp.broadcast_to(t[:, None, :], (HEAD, CHAIN_BATCH, heads)).reshape(HEAD, CHAIN_BATCH * heads)


def _state_to_chains(s):
    b, h = s.shape[0], s.shape[1]
    nt = b // CHAIN_BATCH
    s = s.reshape(nt, CHAIN_BATCH, h, HEAD, HEAD)
    return s.transpose(0, 4, 3, 1, 2).reshape(nt, HEAD, HEAD, CHAIN_BATCH * h)


def _states_from_chains(s, heads):
    n_l, n_d, nt = s.shape[:3]
    s = s.reshape(n_l, n_d, nt, HEAD, HEAD, CHAIN_BATCH, heads)
    return s.transpose(2, 5, 0, 1, 6, 4, 3).reshape(nt * CHAIN_BATCH, n_l, n_d, heads, HEAD, HEAD)


def _block_diag(w, per):
    h, n, _ = w.shape
    g = h // per
    eye = jnp.eye(per, dtype=w.dtype)
    wg = w.reshape(g, per, n, n)
    return (wg[:, :, :, None, :] * eye[None, :, None, :, None]).reshape(g, per * n, per * n)


def _layer_weights(l, n_layers, heads, w_in, mu_rkv, dec_w0, dec_w1, dec_w2, iclr_w0, iclr_w1, iclr_w2,
                   vres_w0, vres_w1, vres_w2, k_k, k_a, r_k, lnx_g, lnx_b, w_out_a, conv_w, conv_b, gr_w,
                   gr_b, gi_w, gi_b, lru_lambda, w_out_b, w_out):
    d = w_in.shape[1]
    w_a = heads * HEAD
    r_dec, r_iclr, r_vres = dec_w1.shape[-1], iclr_w1.shape[-1], vres_w1.shape[-1]
    assert (r_dec, r_iclr, r_vres) == (96, 96, 64), "LoRA column layout assumes ranks 96/96/64"
    has_vres = l > 0
    vw1 = vres_w1[l - 1] if has_vres else jnp.zeros((d, r_vres), F32)
    used = 2 * r_dec + 2 * r_iclr + r_vres
    w_lora = jnp.concatenate(
        [dec_w1[l, 0], dec_w1[l, 1], iclr_w1[l, 0], iclr_w1[l, 1], vw1,
         jnp.zeros((d, LORA_COLS - used), F32)], axis=1).astype(BF16)
    dec_w2p = jnp.zeros((2, 256, w_a), F32)
    iclr_w2p = jnp.zeros((2, 256, w_a), F32)
    for dd in range(2):
        dec_w2p = dec_w2p.at[dd, r_dec * dd:r_dec * (dd + 1)].set(dec_w2[l, dd])
        o = 2 * r_dec - 128 + r_iclr * dd
        iclr_w2p = iclr_w2p.at[dd, o:o + r_iclr].set(iclr_w2[l, dd])
    lw = dict(
        layer=l, w_main=w_in, w_lora=w_lora, mu=mu_rkv[l], dec_w0=dec_w0[l], dec_w2p=dec_w2p.astype(BF16), iclr_w0=iclr_w0[l],
        iclr_w2p=iclr_w2p.astype(BF16),
        kk_c=_chain_const(k_k[l], heads), ka_c=_chain_const(k_a[l], heads),
        rk_c=_chain_const(r_k[l].reshape(-1), heads), g_c=_chain_const(lnx_g[l], heads),
        b_c=_chain_const(lnx_b[l], heads),
        conv_w=conv_w[l], conv_b=conv_b[l].reshape(1, -1),
        gr_bd=jnp.stack([_block_diag(gr_w[l, dd], CB_LRU // gr_w.shape[-1]) for dd in range(2)]).astype(BF16),
        gi_bd=jnp.stack([_block_diag(gi_w[l, dd], CB_LRU // gi_w.shape[-1]) for dd in range(2)]).astype(BF16),
        gr_b=gr_b[l], gi_b=gi_b[l], lam=lru_lambda[l],
        w_out_a=w_out_a, w_out_b=w_out_b, w_out=w_out,
    )
    if has_vres:
        vres_w2p = jnp.zeros((128, w_a), F32).at[0:r_vres].set(vres_w2[l - 1])
        lw.update(vres_w0=vres_w0[l - 1].reshape(1, -1), vres_w2p=vres_w2p.astype(BF16))
    return lw


def _trunk(x, mods, mod_row_in, mod_row_out, tm_in, tm_out, grid_w, rwkv_init, lru_init, layers, final_g,
           heads, w_b):
    b, t, d = x.shape
    w_a = heads * HEAD
    n_layers = len(layers)
    x2 = x.reshape(b * t, d)
    v_first = None
    s_out, h_out = [], []
    for l, lw in enumerate(layers):
        z2, zl = _inproj(x2, mods[l], lw["norm_g"], lw["w_main"], l, lw["w_lora"], tm_in, mod_row_in)
        z3 = z2.reshape(b, t, z2.shape[1])
        r, k, v, lw_f, lw_bk, a_f, a_bk = _prep(z3, zl.reshape(b, t, LORA_COLS), v_first, lw, grid_w, w_a)
        if v_first is None:
            v_first = v
        h_sum, h_fin = _lru(z3, lru_init[l], lw, w_a, w_b)
        rc, kc, vc = (_to_chains(q, heads) for q in (r, k, v))
        consts = (lw["kk_c"], lw["ka_c"], lw["rk_c"])
        y_f, bo_f, s_f = _scan(rc, kc, vc, _to_chains(lw_f, heads), _to_chains(a_f, heads), *consts,
                               rwkv_init[l][0], False)
        y_mix, _, s_b = _scan(rc, kc, vc, _to_chains(lw_bk, heads), _to_chains(a_bk, heads), *consts,
                              rwkv_init[l][1], True, merge=(y_f, bo_f, lw["g_c"], lw["b_c"]))
        y_rwkv = _from_chains(y_mix, heads)
        x2 = _out(y_rwkv.reshape(b * t, w_a), z2, h_sum.reshape(b * t, w_b), x2, mods[l], lw, final_g,
                  tm_out, mod_row_out, l == n_layers - 1, w_a, w_b)
        s_out.append((s_f, s_b))
        h_out.append(h_fin)
    return x2.reshape(b, t, d), s_out, h_out


def kernel(x_prompt, x_sample, c, state_rwkv, state_lru, c_ctx, norm_g, ada_w, ada_b, w_in, mu_rkv, dec_w0,
           dec_w1, dec_w2, iclr_w0, iclr_w1, iclr_w2, vres_w0, vres_w1, vres_w2, k_k, k_a, r_k, lnx_g, lnx_b,
           w_out_a, conv_w, conv_b, gr_w, gr_b, gi_w, gi_b, lru_lambda, w_out_b, w_out, final_g):
    n_layers, d = norm_g.shape
    bp, tp, _ = x_prompt.shape
    bs, ts, _ = x_sample.shape
    heads = r_k.shape[1]
    w_b = conv_w.shape[-1]
    grid_w = int(round(ts ** 0.5))
    assert grid_w * grid_w == ts and heads * CHAIN_BATCH == LANES
    assert bp % CHAIN_BATCH == 0 and bs % CHAIN_BATCH == 0

    w_in, w_out_a, w_out_b, w_out = (w.astype(BF16) for w in (w_in, w_out_a, w_out_b, w_out))
    layers = []
    for l in range(n_layers):
        lw = _layer_weights(l, n_layers, heads, w_in, mu_rkv, dec_w0, dec_w1, dec_w2, iclr_w0, iclr_w1,
                            iclr_w2, vres_w0, vres_w1, vres_w2, k_k, k_a, r_k, lnx_g, lnx_b, w_out_a, conv_w,
                            conv_b, gr_w, gr_b, gi_w, gi_b, lru_lambda, w_out_b, w_out)
        lw["norm_g"] = norm_g[l]
        layers.append(lw)

    ctx_row = bs
    n_rows = -(-(bs + 1) // SUBLANES) * SUBLANES
    cond = jnp.zeros((n_rows, d), F32).at[0:bs].set(c).at[ctx_row].set(c_ctx)
    mod = _ada_mod(cond, ada_w, ada_b).reshape(n_layers, n_rows, 3, d)
    mods = [mod[l] for l in range(n_layers)]

    zero_s = jnp.zeros((bp // CHAIN_BATCH, HEAD, HEAD, LANES), F32)
    tm_p = min(TM_IN, bp * tp)
    tm_po = min(TM_OUT, bp * tp)
    x_ctx, s_ctx, h_ctx = _trunk(
        x_prompt, mods, lambda i: ctx_row, lambda i: ctx_row, tm_p, tm_po, 0,
        [(zero_s, zero_s)] * n_layers, [jnp.zeros((bp, 2, w_b), F32)] * n_layers, layers, final_g, heads, w_b)

    tm_s = min(TM_IN, ts)
    tm_so = min(TM_OUT, ts)
    assert ts % tm_s == 0 and ts % tm_so == 0
    rwkv_init = [tuple(_state_to_chains(state_rwkv[:, l, dd]) for dd in range(2)) for l in range(n_layers)]
    x_lat, _, _ = _trunk(
        x_sample, mods, lambda i: i // (ts // tm_s), lambda i: i // (ts // tm_so), tm_s, tm_so, grid_w,
        rwkv_init, [state_lru[:, l] for l in range(n_layers)], layers, final_g, heads, w_b)

    new_rwkv = _states_from_chains(
        jnp.stack([jnp.stack([s_ctx[l][dd] for dd in range(2)]) for l in range(n_layers)]), heads)
    new_lru = jnp.stack(h_ctx, axis=1)
    return x_ctx, x_lat, new_rwkv, new_lru
```

```python
import functools
import math

import jax
import jax.numpy as jnp
from jax import lax
from jax.experimental import pallas as pl
from jax.experimental.pallas import tpu as pltpu

F32 = jnp.float32
BF16 = jnp.bfloat16

LANES = 128
SUBLANES = 8
HEAD = 64
CHAIN_BATCH = 8
LORA_COLS = 512
RMS_EPS = 1e-6
GN_EPS = 64e-5
LRU_C = 8.0
KK_EPS = 1e-12
DECAY_SCALE = math.exp(-0.5) * math.log2(math.e)
VMEM_LIMIT = 56 * 1024 * 1024

TM_IN = 1024
TN_IN = 2048
ROWS_NORM = 16
TM_OUT = 256
TT_PREP = 512
ROWS_PREP = 64
TILE_PREP = 512 * 512
TB_SCAN = 64
ROWS_SCAN = 64
CB_LRU = 256
CH_LRU = 512


def _params(*sem):
    return pltpu.CompilerParams(dimension_semantics=sem, vmem_limit_bytes=VMEM_LIMIT)


def _cost(flops, transcendentals, *arrays):
    nbytes = sum(math.prod(a.shape) * jnp.dtype(a.dtype).itemsize for a in arrays)
    return pl.CostEstimate(flops=int(flops), transcendentals=int(transcendentals), bytes_accessed=int(nbytes))


def _softplus(x):
    return jnp.maximum(x, 0.0) + jnp.log1p(jnp.exp(-jnp.abs(x)))


def _sigmoid(x):
    return 0.5 * jnp.tanh(0.5 * x) + 0.5


def _silu(x):
    return x * _sigmoid(x)


def _ada_kernel(c_ref, w_ref, b_ref, o_ref):
    h = _silu(c_ref[...]).astype(BF16)
    o_ref[0] = jnp.dot(h, w_ref[0].astype(BF16), preferred_element_type=F32) + b_ref[0]


def _ada_mod(cond, ada_w, ada_b):
    n_l, d, n = ada_w.shape
    bc = cond.shape[0]
    tn = math.gcd(n, 512)
    return pl.pallas_call(
        _ada_kernel,
        grid=(n_l, n // tn),
        in_specs=[
            pl.BlockSpec((bc, d), lambda l, j: (0, 0)),
            pl.BlockSpec((1, d, tn), lambda l, j: (l, 0, j)),
            pl.BlockSpec((1, 1, tn), lambda l, j: (l, 0, j)),
        ],
        out_specs=pl.BlockSpec((1, bc, tn), lambda l, j: (l, 0, j)),
        out_shape=jax.ShapeDtypeStruct((n_l, bc, n), F32),
        compiler_params=_params("arbitrary", "arbitrary"),
        name="ada",
    )(cond, ada_w, ada_b.reshape(n_l, 1, n))


def _norm_kernel(x_ref, mod_ref, g_ref, xm_ref):
    gain = g_ref[...]
    scale = 1.0 + mod_ref[0, 1:2, :]
    shift = mod_ref[0, 0:1, :]

    def rows_body(c, carry):
        rows = pl.ds(pl.multiple_of(c * ROWS_NORM, ROWS_NORM), ROWS_NORM)
        x = x_ref[rows, :]
        y = x * lax.rsqrt(jnp.mean(x * x, axis=-1, keepdims=True) + RMS_EPS) * gain
        xm_ref[rows, :] = (y * scale + shift).astype(BF16)
        return carry

    lax.fori_loop(0, x_ref.shape[0] // ROWS_NORM, rows_body, 0, unroll=8)


def _norm(x2, mod, norm_g, tm, mod_row):
    m, d = x2.shape
    out = jax.ShapeDtypeStruct((m, d), BF16)
    return pl.pallas_call(
        _norm_kernel,
        grid=(m // tm,),
        in_specs=[
            pl.BlockSpec((tm, d), lambda i: (i, 0)),
            pl.BlockSpec((1, 3, d), lambda i: (mod_row(i), 0, 0)),
            pl.BlockSpec((1, d), lambda i: (0, 0)),
        ],
        out_specs=pl.BlockSpec((tm, d), lambda i: (i, 0)),
        out_shape=out,
        compiler_params=_params("arbitrary"),
        cost_estimate=_cost(8 * m * d, m, x2, out),
        name="norm",
    )(x2, mod, norm_g.reshape(1, d))


def _inproj_kernel(xm_ref, w_ref, wl_ref, o_ref, ol_ref):
    o_ref[...] = jnp.dot(xm_ref[...], w_ref[...], preferred_element_type=F32)

    @pl.when(pl.program_id(1) == pl.num_programs(1) - 1)
    def _():
        ol_ref[...] = jnp.dot(xm_ref[...], wl_ref[...], preferred_element_type=F32)


def _col_tile(n, cap):
    return 256 * max(q for q in range(1, cap // 256 + 1) if (n // 256) % q == 0)


def _inproj(xm, w_main, layer, w_lora, tm):
    m, d = xm.shape
    n = w_main.shape[2]
    nl = w_lora.shape[1]
    tn = _col_tile(n, TN_IN)
    outs = [jax.ShapeDtypeStruct((m, n), F32), jax.ShapeDtypeStruct((m, nl), F32)]
    return pl.pallas_call(
        _inproj_kernel,
        grid=(m // tm, n // tn),
        in_specs=[
            pl.BlockSpec((tm, d), lambda i, j: (i, 0)),
            pl.BlockSpec((None, d, tn), lambda i, j: (layer, 0, j)),
            pl.BlockSpec((d, nl), lambda i, j: (0, 0)),
        ],
        out_specs=[pl.BlockSpec((tm, tn), lambda i, j: (i, j)), pl.BlockSpec((tm, nl), lambda i, j: (i, 0))],
        out_shape=outs,
        compiler_params=_params("arbitrary", "arbitrary"),
        cost_estimate=_cost(2 * m * d * (n + nl), 0, xm, jax.ShapeDtypeStruct((d, n), BF16), w_lora, *outs),
        name="inproj",
    )(xm, w_main, w_lora)


def _prep_kernel(*refs, tt, cw, grid_w, n_t, has_vres):
    it = iter(refs)
    zr, zk, zv = next(it), next(it), next(it)
    halos = [next(it) for _ in range(6)] if grid_w else [None] * 6
    lo_ref = next(it)
    vfirst = next(it) if has_vres else None
    mu, dw0, dw2, iw0, iw2 = (next(it) for _ in range(5))
    vw0, vw2 = (next(it), next(it)) if has_vres else (None, None)
    r_o, k_o, v_o, wf_o, wb_o, af_o, ab_o = (next(it) for _ in range(7))
    td_s, li_s, lv_s, er_s, ek_s, ev_s = (next(it) for _ in range(6))

    @pl.when(pl.program_id(2) == 0)
    def _():
        td_s[...] = jnp.tanh(lo_ref[0, :, 0:256]).astype(BF16)
        li_s[...] = lo_ref[0, :, 128:384].astype(BF16)
        lv_s[...] = lo_ref[0, :, 384:512].astype(BF16)

    pad = grid_w if grid_w else SUBLANES
    for e_s, z_ref, hp, hn in ((er_s, zr, halos[0], halos[1]), (ek_s, zk, halos[2], halos[3]),
                               (ev_s, zv, halos[4], halos[5])):
        if grid_w:
            i = pl.program_id(1)
            e_s[0:pad, :] = jnp.where(i > 0, hp[0], 0.0)
            e_s[pad + tt:2 * pad + tt, :] = jnp.where(i < n_t - 1, hn[0], 0.0)
        else:
            e_s[0:pad, :] = jnp.zeros((pad, cw), F32)
            e_s[pad + tt:2 * pad + tt, :] = jnp.zeros((pad, cw), F32)
        e_s[pad:pad + tt, :] = z_ref[0]

    rc = min(tt, ROWS_PREP)
    col = lax.rem(lax.broadcasted_iota(jnp.int32, (rc, cw), 0), grid_w) if grid_w else None

    def mix(e_s, c0, j):
        z = e_s[pad + c0:pad + c0 + rc, :]
        before = e_s[pad + c0 - 1:pad + c0 - 1 + rc, :]
        after = e_s[pad + c0 + 1:pad + c0 + 1 + rc, :]
        if grid_w:
            up = e_s[c0:c0 + rc, :]
            down = e_s[2 * pad + c0:2 * pad + c0 + rc, :]
            left = jnp.where(col == 0, 0.0, before)
            right = jnp.where(col == grid_w - 1, 0.0, after)
            sh = 0.25 * (up + down + left + right)
        else:
            sh = 0.5 * (before + after)
        return z + (sh - z) * mu[j:j + 1, :]

    for c0 in range(0, tt, rc):
        rows = pl.ds(c0, rc)
        r_o[0, rows, :] = mix(er_s, c0, 0)
        k_o[0, rows, :] = mix(ek_s, c0, 1)
        v = mix(ev_s, c0, 2)
        if has_vres:
            gate = _sigmoid(vw0[...] + jnp.dot(lv_s[rows, :], vw2[...], preferred_element_type=F32))
            v = v + (vfirst[0, rows, :] - v) * gate
        v_o[0, rows, :] = v
        td = td_s[rows, :]
        li = li_s[rows, :]
        for d, (w_o, a_o) in enumerate(((wf_o, af_o), (wb_o, ab_o))):
            xd = dw0[d:d + 1, :] + jnp.dot(td, dw2[d], preferred_element_type=F32)
            w_o[0, rows, :] = -DECAY_SCALE * _sigmoid(xd)
            a_o[0, rows, :] = _sigmoid(iw0[d:d + 1, :] + jnp.dot(li, iw2[d], preferred_element_type=F32))


def _prep(z3, zl3, v_first, lw, grid_w, w_a):
    b, t, _ = z3.shape
    has_vres = v_first is not None
    if grid_w:
        tt = min(TT_PREP, t)
        assert min(tt, ROWS_PREP) % grid_w == 0 and grid_w % SUBLANES == 0 and t % tt == 0
    else:
        tt = t
    n_t = t // tt
    cw = min(w_a, max(LANES, TILE_PREP // tt // LANES * LANES))
    assert w_a % cw == 0
    nc = w_a // cw
    blk = lambda off: pl.BlockSpec((1, tt, cw), lambda bb, i, c, off=off: (bb, i, off + c))
    in_specs = [blk(0), blk(nc), blk(2 * nc)]
    args = [z3, z3, z3]
    if grid_w:
        per, last = tt // grid_w, t // grid_w - 1
        for off in (0, nc, 2 * nc):
            in_specs.append(pl.BlockSpec(
                (1, grid_w, cw),
                lambda bb, i, c, off=off: (bb, jnp.maximum(i * per - 1, 0), off + c)))
            in_specs.append(pl.BlockSpec(
                (1, grid_w, cw),
                lambda bb, i, c, off=off: (bb, jnp.minimum((i + 1) * per, last), off + c)))
            args += [z3, z3]
    in_specs.append(pl.BlockSpec((1, tt, LORA_COLS), lambda bb, i, c: (bb, i, 0)))
    args.append(zl3)
    if has_vres:
        in_specs.append(blk(0))
        args.append(v_first)
    wcol = lambda shape: pl.BlockSpec(shape, lambda bb, i, c: (0,) * (len(shape) - 1) + (c,))
    in_specs += [wcol((3, cw)), wcol((2, cw)), wcol((2, 256, cw)), wcol((2, cw)), wcol((2, 256, cw))]
    args += [lw["mu"], lw["dec_w0"], lw["dec_w2p"], lw["iclr_w0"], lw["iclr_w2p"]]
    if has_vres:
        in_specs += [wcol((1, cw)), wcol((128, cw))]
        args += [lw["vres_w0"], lw["vres_w2p"]]
    out = jax.ShapeDtypeStruct((b, t, w_a), F32)
    return pl.pallas_call(
        functools.partial(_prep_kernel, tt=tt, cw=cw, grid_w=grid_w, n_t=n_t, has_vres=has_vres),
        grid=(b, n_t, nc),
        in_specs=in_specs,
        out_specs=[blk(0)] * 7,
        out_shape=[out] * 7,
        scratch_shapes=[pltpu.VMEM((tt, 256), BF16), pltpu.VMEM((tt, 256), BF16), pltpu.VMEM((tt, 128), BF16)]
        + [pltpu.VMEM((tt + 2 * (grid_w if grid_w else SUBLANES), cw), F32)] * 3,
        compiler_params=_params("arbitrary", "arbitrary", "arbitrary"),
        cost_estimate=_cost(b * t * w_a * (2 * 5 * 256 + 60), 6 * b * t * w_a, *[out] * 11),
        name="prep",
    )(*args)


def _scan_kernel(*refs, tb, reverse, merge):
    it = iter(refs)
    r_ref, k_ref, v_ref, lw_ref, a_ref, kkc_ref, kac_ref, rkc_ref, s0_ref = (next(it) for _ in range(9))
    yo_ref, boo_ref, g_ref, b_ref = (next(it) for _ in range(4)) if merge else (None,) * 4
    y_ref, bo_ref, sfin_ref, s_scr, al_s, be_s, kt_s, rt_s = (next(it) for _ in range(8))
    assert not merge or ROWS_SCAN == HEAD, "the merged group norm needs a whole head per row group"
    i = pl.program_id(1)

    @pl.when(i == 0)
    def _():
        s_scr[...] = s0_ref[0]

    off = 1 if reverse else 0
    al_s[0 if reverse else tb] = jnp.zeros((HEAD, LANES), F32)

    kac = kac_ref[...]
    one_minus_kac = 1.0 - kac

    def prep_step(s, carry):
        lcum, wprev = carry
        t = (tb - 1 - s) if reverse else s
        k_t, a_t, r_t = k_ref[0, t], a_ref[0, t], r_ref[0, t]
        kkr = k_t * kkc_ref[...]
        nrm = jnp.sqrt(jnp.sum(kkr * kkr, axis=0, keepdims=True))
        kk = kkr * (1.0 / jnp.maximum(nrm, KK_EPS))
        al_s[t + off] = wprev * kk
        lcum = lcum + lw_ref[0, t]
        wcum = jnp.exp2(lcum)
        inv = jnp.exp2(-lcum)
        kd = k_t * (a_t * kac + one_minus_kac)
        be_s[t] = (kk * a_t) * inv
        kt_s[t] = kd * inv
        rt_s[t] = wcum * r_t
        bo = jnp.sum((r_t * kd) * rkc_ref[...], axis=0, keepdims=True)
        bo_ref[0, t] = jnp.broadcast_to(bo, (SUBLANES, LANES))
        return lcum, wcum

    _, w_last = lax.fori_loop(0, tb, prep_step, (jnp.zeros((HEAD, LANES), F32), jnp.ones((HEAD, LANES), F32)),
                              unroll=4)

    groups = [pl.ds(g * ROWS_SCAN, ROWS_SCAN) for g in range(HEAD // ROWS_SCAN)]

    def s_dot_al(rows, slot):
        acc = [None, None]
        for kx in range(HEAD):
            term = s_scr[kx, rows, :] * al_s[slot, pl.ds(kx, 1), :]
            acc[kx % 2] = term if acc[kx % 2] is None else acc[kx % 2] + term
        return acc[0] + acc[1]

    def time_of(s):
        return (tb - 1 - s) if reverse else s

    def finish(s):
        t = time_of(s)
        y = yo_ref[0, t] + y_ref[0, t]
        dlt = y - jnp.mean(y, axis=0, keepdims=True)
        var = jnp.mean(dlt * dlt, axis=0, keepdims=True)
        bo = boo_ref[0, t, 0:1, :] + bo_ref[0, t, 0:1, :]
        y_ref[0, t] = dlt * lax.rsqrt(var + GN_EPS) * g_ref[...] + b_ref[...] + bo * v_ref[0, t]

    def step(s, sal_all):
        t = time_of(s)
        nxt = t if reverse else t + 1
        sal_next = []
        for rows, sal in zip(groups, sal_all):
            vv = v_ref[0, t, rows, :]
            yac = acc = None
            for kx in range(HEAD):
                kr = pl.ds(kx, 1)
                sn = s_scr[kx, rows, :] - sal * be_s[t, kr, :] + vv * kt_s[t, kr, :]
                s_scr[kx, rows, :] = sn
                ty = sn * rt_s[t, kr, :]
                ta = sn * al_s[nxt, kr, :]
                yac = ty if yac is None else yac + ty
                acc = ta if acc is None else acc + ta
            y_ref[0, t, rows, :] = yac
            sal_next.append(acc)
        return tuple(sal_next)

    def step_and_finish(s, sal_all):
        finish(s - 1)
        return step(s, sal_all)

    first = (tb - 1 + off) if reverse else 0
    sal0 = tuple(s_dot_al(rows, first) for rows in groups)
    if merge:
        lax.fori_loop(1, tb, step_and_finish, step(0, sal0))
        finish(tb - 1)
    else:
        lax.fori_loop(0, tb, step, sal0)

    wl_s = be_s.at[0]
    wl_s[...] = w_last
    for kx in range(HEAD):
        s_scr[kx] = s_scr[kx] * wl_s[pl.ds(kx, 1), :]

    @pl.when(i == pl.num_programs(1) - 1)
    def _():
        sfin_ref[0] = s_scr[...]


def _scan(r, k, v, lw, a, kk_c, ka_c, rk_c, s0, reverse, merge=()):
    nt, t = r.shape[0], r.shape[1]
    tb = min(TB_SCAN, t)
    nb = t // tb
    tmap = (lambda c, i: (c, nb - 1 - i, 0, 0)) if reverse else (lambda c, i: (c, i, 0, 0))
    seq = pl.BlockSpec((1, tb, HEAD, LANES), tmap)
    row = pl.BlockSpec((1, tb, SUBLANES, LANES), tmap)
    cst = pl.BlockSpec((HEAD, LANES), lambda c, i: (0, 0))
    st = pl.BlockSpec((1, HEAD, HEAD, LANES), lambda c, i: (c, 0, 0, 0))
    return pl.pallas_call(
        functools.partial(_scan_kernel, tb=tb, reverse=reverse, merge=bool(merge)),
        grid=(nt, nb),
        in_specs=[seq] * 5 + [cst, cst, cst, st] + ([seq, row, cst, cst] if merge else []),
        out_specs=[seq, row, st],
        out_shape=[jax.ShapeDtypeStruct(r.shape, F32), jax.ShapeDtypeStruct((nt, t, SUBLANES, LANES), F32),
                   jax.ShapeDtypeStruct(s0.shape, F32)],
        scratch_shapes=[pltpu.VMEM((HEAD, HEAD, LANES), F32), pltpu.VMEM((tb + 1, HEAD, LANES), F32)]
        + [pltpu.VMEM((tb, HEAD, LANES), F32)] * 3,
        compiler_params=_params("arbitrary", "arbitrary"),
        cost_estimate=_cost(8 * r.size * HEAD, 4 * r.size, r, k, v, lw, a, r, s0, s0, *merge[:1]),
        name="scan_bwd" if reverse else "scan_fwd",
    )(r, k, v, lw, a, kk_c, ka_c, rk_c, s0, *merge)


def _lru_kernel(x_ref, h0_ref, cw_ref, cb_ref, wr_ref, br_ref, wi_ref, bi_ref, lam_ref, o_ref, hfin_ref,
                xpad, af, uf, ab, ub, *, t_len, ch):
    cbk = x_ref.shape[-1]
    pad = SUBLANES
    xpad[0:pad, :] = jnp.zeros((pad, cbk), F32)
    xpad[t_len + pad:t_len + 2 * pad, :] = jnp.zeros((pad, cbk), F32)
    xpad[pad:t_len + pad, :] = x_ref[0]
    cw = cw_ref[...]
    for c0 in range(0, t_len, ch):
        xc = cb_ref[...] + xpad[c0 + pad - 2:c0 + pad - 2 + ch, :] * cw[0:1, :]
        for j in range(1, 4):
            xc = xc + xpad[c0 + pad - 2 + j:c0 + pad - 2 + j + ch, :] * cw[j:j + 1, :]
        xcb = xc.astype(BF16)
        for d, (a_s, u_s) in enumerate(((af, uf), (ab, ub))):
            rg = _sigmoid(jnp.dot(xcb, wr_ref[d, 0], preferred_element_type=F32) + br_ref[d:d + 1, :])
            ig = _sigmoid(jnp.dot(xcb, wi_ref[d, 0], preferred_element_type=F32) + bi_ref[d:d + 1, :])
            log_a = (-LRU_C * _softplus(-lam_ref[d:d + 1, :])) * rg
            a = jnp.exp(log_a)
            a_s[c0:c0 + ch, :] = a
            s2 = 1.0 - a * a
            root = jnp.where(s2 > 0.0, s2 * lax.rsqrt(s2), 0.0)
            u_s[c0:c0 + ch, :] = root * (ig * xc)

    rowi = lax.broadcasted_iota(jnp.int32, (SUBLANES, cbk), 0)

    def tile_scan(a, u, h_in, backward):
        for s in (1, 2, 4):
            sh = SUBLANES - s if backward else s
            a_s, u_s = pltpu.roll(a, sh, 0), pltpu.roll(u, sh, 0)
            m = (rowi < SUBLANES - s) if backward else (rowi >= s)
            u = jnp.where(m, u + a * u_s, u)
            a = jnp.where(m, a * a_s, a)
        h = a * h_in + u
        return h, (h[0:1, :] if backward else h[SUBLANES - 1:SUBLANES, :])

    def body(i, carry):
        hf, hb = carry
        of = pl.multiple_of(i * SUBLANES, SUBLANES)
        h, hf = tile_scan(af[pl.ds(of, SUBLANES), :], uf[pl.ds(of, SUBLANES), :], hf, False)
        af[pl.ds(of, SUBLANES), :] = h
        ob = pl.multiple_of(t_len - SUBLANES - i * SUBLANES, SUBLANES)
        h, hb = tile_scan(ab[pl.ds(ob, SUBLANES), :], ub[pl.ds(ob, SUBLANES), :], hb, True)
        ab[pl.ds(ob, SUBLANES), :] = h
        return hf, hb

    hf, hb = lax.fori_loop(0, t_len // SUBLANES, body, (h0_ref[0, 0:1, :], h0_ref[0, 1:2, :]), unroll=8)
    o_ref[0] = af[...] + ab[...]
    hfin_ref[0, 0:1, :] = hf
    hfin_ref[0, 1:2, :] = hb


def _lru(z3, h0, lw, w_a, w_b):
    b, t, _ = z3.shape
    cbk = CB_LRU
    ncb = w_b // cbk
    off = 4 * w_a // cbk
    ch = min(CH_LRU, t)
    wcol = lambda rows: pl.BlockSpec((rows, cbk), lambda bb, c: (0, c))
    wgate = pl.BlockSpec((2, 1, cbk, cbk), lambda bb, c: (0, c, 0, 0))
    st = pl.BlockSpec((1, 2, cbk), lambda bb, c: (bb, 0, c))
    return pl.pallas_call(
        functools.partial(_lru_kernel, t_len=t, ch=ch),
        grid=(b, ncb),
        in_specs=[pl.BlockSpec((1, t, cbk), lambda bb, c: (bb, 0, off + c)), st,
                  wcol(4), wcol(1), wgate, wcol(2), wgate, wcol(2), wcol(2)],
        out_specs=[pl.BlockSpec((1, t, cbk), lambda bb, c: (bb, 0, c)), st],
        out_shape=[jax.ShapeDtypeStruct((b, t, w_b), F32), jax.ShapeDtypeStruct((b, 2, w_b), F32)],
        scratch_shapes=[pltpu.VMEM((t + 2 * SUBLANES, cbk), F32)] + [pltpu.VMEM((t, cbk), F32)] * 4,
        compiler_params=_params("arbitrary", "arbitrary"),
        cost_estimate=_cost(b * t * w_b * (8 * cbk + 150), 8 * b * t * w_b,
                            *[jax.ShapeDtypeStruct((b, t, w_b), F32)] * 2),
        name="lru",
    )(z3, h0, lw["conv_w"], lw["conv_b"], lw["gr_bd"], lw["gr_b"], lw["gi_bd"], lw["gi_b"], lw["lam"])


def _out_kernel(y_ref, ga_ref, h_ref, gb_ref, ma_ref, mb_ref, x_ref, mod_ref, woa_ref, wob_ref, wo_ref,
                g_ref, modn_ref, o_ref, *xm_ref, final):
    ya = jnp.dot((y_ref[...] * _silu(ga_ref[...])).astype(BF16), woa_ref[...], preferred_element_type=F32)
    yb = jnp.dot((h_ref[...] * _silu(gb_ref[...])).astype(BF16), wob_ref[...], preferred_element_type=F32)
    merged = _sigmoid(ma_ref[...]) * ya + _sigmoid(mb_ref[...]) * yb
    out = jnp.dot(merged.astype(BF16), wo_ref[...], preferred_element_type=F32)
    xn = x_ref[...] + mod_ref[0, 2:3, :] * out
    normed = xn * lax.rsqrt(jnp.mean(xn * xn, axis=-1, keepdims=True) + RMS_EPS) * g_ref[...]
    if final:
        o_ref[...] = normed
    else:
        o_ref[...] = xn
        xm_ref[0][...] = (normed * (1.0 + modn_ref[0, 1:2, :]) + modn_ref[0, 0:1, :]).astype(BF16)


def _out(y2, z2, h2, x2, mod, lw, gain, mod_next, tm, mod_row, final, w_a, w_b):
    m, d = x2.shape
    layer = lw["layer"]
    once = dict(pipeline_mode=pl.Buffered(1))
    zoff_ga = 3 * w_a // w_a
    zoff_gb = (4 * w_a + w_b) // w_b
    zoff_ma = (4 * w_a + 2 * w_b) // d
    tile = pl.BlockSpec((tm, d), lambda i: (i, 0))
    res = jax.ShapeDtypeStruct((m, d), F32)
    return pl.pallas_call(
        functools.partial(_out_kernel, final=final),
        grid=(m // tm,),
        in_specs=[
            pl.BlockSpec((tm, w_a), lambda i: (i, 0)),
            pl.BlockSpec((tm, w_a), lambda i: (i, zoff_ga)),
            pl.BlockSpec((tm, w_b), lambda i: (i, 0)),
            pl.BlockSpec((tm, w_b), lambda i: (i, zoff_gb)),
            pl.BlockSpec((tm, d), lambda i: (i, zoff_ma)),
            pl.BlockSpec((tm, d), lambda i: (i, zoff_ma + 1)),
            pl.BlockSpec((tm, d), lambda i: (i, 0)),
            pl.BlockSpec((1, 3, d), lambda i: (mod_row(i), 0, 0)),
            pl.BlockSpec((None, w_a, d), lambda i: (layer, 0, 0), **once),
            pl.BlockSpec((None, w_b, d), lambda i: (layer, 0, 0), **once),
            pl.BlockSpec((None, d, d), lambda i: (layer, 0, 0), **once),
            pl.BlockSpec((1, d), lambda i: (0, 0)),
            pl.BlockSpec((1, 3, d), lambda i: (mod_row(i), 0, 0)),
        ],
        out_specs=tile if final else [tile, tile],
        out_shape=res if final else [res, jax.ShapeDtypeStruct((m, d), BF16)],
        compiler_params=_params("arbitrary"),
        cost_estimate=_cost(2 * m * d * (w_a + w_b + d), 4 * m * d, y2, y2, h2, h2, x2, x2, x2, x2, x2),
        name="out_final" if final else "out",
    )(y2, z2, h2, z2, z2, z2, x2, mod, lw["w_out_a"], lw["w_out_b"], lw["w_out"], gain.reshape(1, d), mod_next)


def _to_chains(x, heads):
    b, t, _ = x.shape
    nt = b // CHAIN_BATCH
    y = jnp.swapaxes(x.reshape(nt, CHAIN_BATCH, t, heads * HEAD), 1, 2)
    return jnp.swapaxes(y.reshape(nt, t, CHAIN_BATCH * heads, HEAD), 2, 3)


def _from_chains(x, heads):
    nt, t = x.shape[0], x.shape[1]
    y = jnp.swapaxes(x, 2, 3).reshape(nt, t, CHAIN_BATCH, heads * HEAD)
    return jnp.swapaxes(y, 1, 2).reshape(nt * CHAIN_BATCH, t, heads * HEAD)


def _chain_const(p, heads):
    t = p.reshape(heads, HEAD).T
    return jnp.broadcast_to(t[:, None, :], (HEAD, CHAIN_BATCH, heads)).reshape(HEAD, CHAIN_BATCH * heads)


def _state_to_chains(s):
    b, h = s.shape[0], s.shape[1]
    nt = b // CHAIN_BATCH
    s = s.reshape(nt, CHAIN_BATCH, h, HEAD, HEAD)
    return s.transpose(0, 4, 3, 1, 2).reshape(nt, HEAD, HEAD, CHAIN_BATCH * h)


def _states_from_chains(s, heads):
    n_l, n_d, nt = s.shape[:3]
    s = s.reshape(n_l, n_d, nt, HEAD, HEAD, CHAIN_BATCH, heads)
    return s.transpose(2, 5, 0, 1, 6, 4, 3).reshape(nt * CHAIN_BATCH, n_l, n_d, heads, HEAD, HEAD)


def _block_diag(w, per):
    h, n, _ = w.shape
    g = h // per
    eye = jnp.eye(per, dtype=w.dtype)
    wg = w.reshape(g, per, n, n)
    return (wg[:, :, :, None, :] * eye[None, :, None, :, None]).reshape(g, per * n, per * n)


def _layer_weights(l, n_layers, heads, w_in, mu_rkv, dec_w0, dec_w1, dec_w2, iclr_w0, iclr_w1, iclr_w2,
                   vres_w0, vres_w1, vres_w2, k_k, k_a, r_k, lnx_g, lnx_b, w_out_a, conv_w, conv_b, gr_w,
                   gr_b, gi_w, gi_b, lru_lambda, w_out_b, w_out):
    d = w_in.shape[1]
    w_a = heads * HEAD
    r_dec, r_iclr, r_vres = dec_w1.shape[-1], iclr_w1.shape[-1], vres_w1.shape[-1]
    assert (r_dec, r_iclr, r_vres) == (96, 96, 64), "LoRA column layout assumes ranks 96/96/64"
    has_vres = l > 0
    vw1 = vres_w1[l - 1] if has_vres else jnp.zeros((d, r_vres), F32)
    used = 2 * r_dec + 2 * r_iclr + r_vres
    w_lora = jnp.concatenate(
        [dec_w1[l, 0], dec_w1[l, 1], iclr_w1[l, 0], iclr_w1[l, 1], vw1,
         jnp.zeros((d, LORA_COLS - used), F32)], axis=1).astype(BF16)
    dec_w2p = jnp.zeros((2, 256, w_a), F32)
    iclr_w2p = jnp.zeros((2, 256, w_a), F32)
    for dd in range(2):
        dec_w2p = dec_w2p.at[dd, r_dec * dd:r_dec * (dd + 1)].set(dec_w2[l, dd])
        o = 2 * r_dec - 128 + r_iclr * dd
        iclr_w2p = iclr_w2p.at[dd, o:o + r_iclr].set(iclr_w2[l, dd])
    lw = dict(
        layer=l, w_main=w_in, w_lora=w_lora, mu=mu_rkv[l], dec_w0=dec_w0[l], dec_w2p=dec_w2p.astype(BF16), iclr_w0=iclr_w0[l],
        iclr_w2p=iclr_w2p.astype(BF16),
        kk_c=_chain_const(k_k[l], heads), ka_c=_chain_const(k_a[l], heads),
        rk_c=_chain_const(r_k[l].reshape(-1), heads), g_c=_chain_const(lnx_g[l], heads),
        b_c=_chain_const(lnx_b[l], heads),
        conv_w=conv_w[l], conv_b=conv_b[l].reshape(1, -1),
        gr_bd=jnp.stack([_block_diag(gr_w[l, dd], CB_LRU // gr_w.shape[-1]) for dd in range(2)]).astype(BF16),
        gi_bd=jnp.stack([_block_diag(gi_w[l, dd], CB_LRU // gi_w.shape[-1]) for dd in range(2)]).astype(BF16),
        gr_b=gr_b[l], gi_b=gi_b[l], lam=lru_lambda[l],
        w_out_a=w_out_a, w_out_b=w_out_b, w_out=w_out,
    )
    if has_vres:
        vres_w2p = jnp.zeros((128, w_a), F32).at[0:r_vres].set(vres_w2[l - 1])
        lw.update(vres_w0=vres_w0[l - 1].reshape(1, -1), vres_w2p=vres_w2p.astype(BF16))
    return lw


def _trunk(x, mods, mod_row_in, mod_row_out, tm_in, tm_out, grid_w, rwkv_init, lru_init, layers, final_g,
           heads, w_b):
    b, t, d = x.shape
    w_a = heads * HEAD
    n_layers = len(layers)
    x2 = x.reshape(b * t, d)
    v_first = None
    s_out, h_out = [], []
    xm = _norm(x2, mods[0], layers[0]["norm_g"], tm_in, mod_row_in)
    for l, lw in enumerate(layers):
        z2, zl = _inproj(xm, lw["w_main"], l, lw["w_lora"], tm_in)
        z3 = z2.reshape(b, t, z2.shape[1])
        r, k, v, lw_f, lw_bk, a_f, a_bk = _prep(z3, zl.reshape(b, t, LORA_COLS), v_first, lw, grid_w, w_a)
        if v_first is None:
            v_first = v
        h_sum, h_fin = _lru(z3, lru_init[l], lw, w_a, w_b)
        rc, kc, vc = (_to_chains(q, heads) for q in (r, k, v))
        consts = (lw["kk_c"], lw["ka_c"], lw["rk_c"])
        y_f, bo_f, s_f = _scan(rc, kc, vc, _to_chains(lw_f, heads), _to_chains(a_f, heads), *consts,
                               rwkv_init[l][0], False)
        y_mix, _, s_b = _scan(rc, kc, vc, _to_chains(lw_bk, heads), _to_chains(a_bk, heads), *consts,
                              rwkv_init[l][1], True, merge=(y_f, bo_f, lw["g_c"], lw["b_c"]))
        y_rwkv = _from_chains(y_mix, heads)
        last = l == n_layers - 1
        res = _out(y_rwkv.reshape(b * t, w_a), z2, h_sum.reshape(b * t, w_b), x2, mods[l], lw,
                   final_g if last else layers[l + 1]["norm_g"], mods[l if last else l + 1],
                   tm_out, mod_row_out, last, w_a, w_b)
        x2, xm = (res, None) if last else res
        s_out.append((s_f, s_b))
        h_out.append(h_fin)
    return x2.reshape(b, t, d), s_out, h_out


def kernel(x_prompt, x_sample, c, state_rwkv, state_lru, c_ctx, norm_g, ada_w, ada_b, w_in, mu_rkv, dec_w0,
           dec_w1, dec_w2, iclr_w0, iclr_w1, iclr_w2, vres_w0, vres_w1, vres_w2, k_k, k_a, r_k, lnx_g, lnx_b,
           w_out_a, conv_w, conv_b, gr_w, gr_b, gi_w, gi_b, lru_lambda, w_out_b, w_out, final_g):
    n_layers, d = norm_g.shape
    bp, tp, _ = x_prompt.shape
    bs, ts, _ = x_sample.shape
    heads = r_k.shape[1]
    w_b = conv_w.shape[-1]
    grid_w = int(round(ts ** 0.5))
    assert grid_w * grid_w == ts and heads * CHAIN_BATCH == LANES
    assert bp % CHAIN_BATCH == 0 and bs % CHAIN_BATCH == 0

    w_in, w_out_a, w_out_b, w_out = (w.astype(BF16) for w in (w_in, w_out_a, w_out_b, w_out))
    layers = []
    for l in range(n_layers):
        lw = _layer_weights(l, n_layers, heads, w_in, mu_rkv, dec_w0, dec_w1, dec_w2, iclr_w0, iclr_w1,
                            iclr_w2, vres_w0, vres_w1, vres_w2, k_k, k_a, r_k, lnx_g, lnx_b, w_out_a, conv_w,
                            conv_b, gr_w, gr_b, gi_w, gi_b, lru_lambda, w_out_b, w_out)
        lw["norm_g"] = norm_g[l]
        layers.append(lw)

    ctx_row = bs
    n_rows = -(-(bs + 1) // SUBLANES) * SUBLANES
    cond = jnp.zeros((n_rows, d), F32).at[0:bs].set(c).at[ctx_row].set(c_ctx)
    mod = _ada_mod(cond, ada_w, ada_b).reshape(n_layers, n_rows, 3, d)
    mods = [mod[l] for l in range(n_layers)]

    zero_s = jnp.zeros((bp // CHAIN_BATCH, HEAD, HEAD, LANES), F32)
    tm_p = min(TM_IN, bp * tp)
    tm_po = min(TM_OUT, bp * tp)
    x_ctx, s_ctx, h_ctx = _trunk(
        x_prompt, mods, lambda i: ctx_row, lambda i: ctx_row, tm_p, tm_po, 0,
        [(zero_s, zero_s)] * n_layers, [jnp.zeros((bp, 2, w_b), F32)] * n_layers, layers, final_g, heads, w_b)

    tm_s = min(TM_IN, ts)
    tm_so = min(TM_OUT, ts)
    assert ts % tm_s == 0 and ts % tm_so == 0
    rwkv_init = [tuple(_state_to_chains(state_rwkv[:, l, dd]) for dd in range(2)) for l in range(n_layers)]
    x_lat, _, _ = _trunk(
        x_sample, mods, lambda i: i // (ts // tm_s), lambda i: i // (ts // tm_so), tm_s, tm_so, grid_w,
        rwkv_init, [state_lru[:, l] for l in range(n_layers)], layers, final_g, heads, w_b)

    new_rwkv = _states_from_chains(
        jnp.stack([jnp.stack([s_ctx[l][dd] for dd in range(2)]) for l in range(n_layers)]), heads)
    new_lru = jnp.stack(h_ctx, axis=1)
    return x_ctx, x_lat, new_rwkv, new_lru
```
